```python
import jax, jax.numpy as jnp
from jax import lax
import numpy as np

D_MODEL = 2048
BATCH = 8
SEQ = 8192
DEPTH = 4

CONV_CHANNELS = D_MODEL
CONV_WIDTH = 31
HEAD_DIM_K = 128
HEAD_DIM_V = 128
N_HEADS = D_MODEL // 128
KEY_DIM = N_HEADS * HEAD_DIM_K
VALUE_DIM = N_HEADS * HEAD_DIM_V
SHORT_CONV = 4
CHUNK = 64
D_FF = -((-8 * D_MODEL) // (3 * 256)) * 256
DEEPNORM_ALPHA = (2.0 * DEPTH) ** 0.25
DEEPNORM_BETA = (8.0 * DEPTH) ** -0.25
LN_EPS = 1e-5
IN_SPLITS = (CONV_CHANNELS, CONV_CHANNELS, 2 * KEY_DIM + VALUE_DIM, VALUE_DIM,
             N_HEADS, N_HEADS, D_MODEL, D_MODEL)
IN_WIDTH = sum(IN_SPLITS)

kernel_name = "hybrid_conformer_gdn_deepnorm"


def layer_norm(x, g, b):
    xf = x.astype(jnp.float32)
    mu = jnp.mean(xf, axis=-1, keepdims=True)
    var = jnp.mean(jnp.square(xf - mu), axis=-1, keepdims=True)
    y = (xf - mu) * lax.rsqrt(var + LN_EPS) * g.astype(jnp.float32) + b.astype(jnp.float32)
    return y.astype(x.dtype)


def causal_depthwise_conv(x, w):
    k, c = w.shape
    return lax.conv_general_dilated(
        x, w[:, None, :], window_strides=(1,), padding=[(k - 1, 0)],
        dimension_numbers=('NWC', 'WIO', 'NWC'), feature_group_count=c)


def l2_normalize(x):
    return x * lax.rsqrt(jnp.sum(jnp.square(x), axis=-1, keepdims=True) + 1e-6)


def gated_delta_rule(q, k, v, beta, log_a):
    bsz, seq, h, dk = q.shape
    dv = v.shape[-1]
    n = seq // CHUNK

    def chunks(t):
        t = t.reshape((bsz, n, CHUNK, h) + t.shape[3:])
        return jnp.moveaxis(t, 3, 1)

    q = chunks(q) * (dk ** -0.5)
    k, v, beta = chunks(k), chunks(v), chunks(beta)
    g = jnp.cumsum(chunks(log_a), axis=-1)
    causal = jnp.tril(jnp.ones((CHUNK, CHUNK), dtype=bool))
    strict = jnp.tril(jnp.ones((CHUNK, CHUNK), dtype=bool), -1)
    decay = jnp.exp(jnp.where(causal, g[..., :, None] - g[..., None, :], -jnp.inf))

    kb = k * beta[..., None]
    kkt = jnp.einsum('bhnid,bhnjd->bhnij', kb, k) * decay
    a_mat = jnp.eye(CHUNK, dtype=jnp.float32) + jnp.where(strict, kkt, 0.0)
    rhs = jnp.concatenate([v * beta[..., None], kb * jnp.exp(g)[..., None]], axis=-1)
    sol = lax.linalg.triangular_solve(a_mat, rhs, left_side=True, lower=True,
                                      unit_diagonal=True)
    u = sol[..., :dv]
    w = sol[..., dv:]

    attn_intra = jnp.where(causal, jnp.einsum('bhnid,bhnjd->bhnij', q, k) * decay, 0.0)
    q_dec = q * jnp.exp(g)[..., None]
    g_last = g[..., -1]
    k_dec = k * jnp.exp(g_last[..., None] - g)[..., None]

    def step(state, inp):
        u_c, w_c, qd_c, kd_c, at_c, gl_c = inp
        v_new = u_c - jnp.einsum('bhck,bhkv->bhcv', w_c, state)
        o = jnp.einsum('bhck,bhkv->bhcv', qd_c, state) + jnp.einsum('bhcj,bhjv->bhcv', at_c, v_new)
        state = state * jnp.exp(gl_c)[..., None, None] + jnp.einsum('bhck,bhcv->bhkv', kd_c, v_new)
        return state, o

    xs = tuple(jnp.moveaxis(t, 2, 0) for t in (u, w, q_dec, k_dec, attn_intra, g_last))
    s0 = jnp.zeros((bsz, h, dk, dv), dtype=jnp.float32)
    _, o = lax.scan(step, s0, xs)
    return jnp.transpose(o, (1, 0, 3, 2, 4)).reshape(bsz, seq, h, dv)


def mixer(h, w_in, b_in, conv_dw_w, conv_dw_b, conv_ln_g, conv_ln_b, w_conv_proj, b_conv_proj,
          short_conv_w, a_log, dt_bias, gdn_norm_w, w_gdn_proj, w_out):
    bsz, seq, _ = h.shape
    z = h @ w_in + b_in
    offsets = np.cumsum(IN_SPLITS)[:-1].tolist()
    glu_a, glu_b, qkv, zgate, beta_raw, a_raw, gate_a, gate_b = jnp.split(z, offsets, axis=-1)

    c = glu_a * jax.nn.sigmoid(glu_b)
    c = causal_depthwise_conv(c, conv_dw_w) + conv_dw_b
    c = jax.nn.silu(layer_norm(c, conv_ln_g, conv_ln_b))
    y_conv = c @ w_conv_proj + b_conv_proj

    qkv = jax.nn.silu(causal_depthwise_conv(qkv, short_conv_w))
    q, k, v = jnp.split(qkv, [KEY_DIM, 2 * KEY_DIM], axis=-1)
    q = l2_normalize(q.reshape(bsz, seq, N_HEADS, HEAD_DIM_K).astype(jnp.float32))
    k = l2_normalize(k.reshape(bsz, seq, N_HEADS, HEAD_DIM_K).astype(jnp.float32))
    v = v.reshape(bsz, seq, N_HEADS, HEAD_DIM_V).astype(jnp.float32)
    beta = jax.nn.sigmoid(beta_raw.astype(jnp.float32))
    log_a = -jnp.exp(a_log.astype(jnp.float32)) * jax.nn.softplus(
        a_raw.astype(jnp.float32) + dt_bias.astype(jnp.float32))
    o = gated_delta_rule(q, k, v, beta, log_a)
    o = o * lax.rsqrt(jnp.mean(jnp.square(o), axis=-1, keepdims=True) + 1e-6)
    o = o * gdn_norm_w.astype(jnp.float32) * jax.nn.silu(
        zgate.reshape(bsz, seq, N_HEADS, HEAD_DIM_V).astype(jnp.float32))
    y_gdn = o.reshape(bsz, seq, VALUE_DIM).astype(h.dtype) @ w_gdn_proj

    m = jax.nn.sigmoid(gate_a) * y_conv + jax.nn.sigmoid(gate_b) * y_gdn
    return m @ w_out


def swiglu(h, w_ffn_in, w_ffn_out):
    gate, up = jnp.split(h @ w_ffn_in, 2, axis=-1)
    return (jax.nn.silu(gate) * up) @ w_ffn_out


def _fwd_setup_inputs(seed: int = 0) -> dict:
    key = jax.random.key(seed)
    ks = jax.random.split(key, 24)
    f32 = jnp.float32
    L = DEPTH

    def nrm(k, shape, scale):
        return jax.random.normal(k, shape, f32) * scale

    dt = jnp.exp(jax.random.uniform(ks[10], (L, N_HEADS), f32, np.log(1e-3), np.log(1e-1)))
    return {
        "x": jax.random.normal(ks[0], (BATCH, SEQ, D_MODEL), f32),
        "w_in": nrm(ks[1], (L, D_MODEL, IN_WIDTH), D_MODEL ** -0.5),
        "b_in": nrm(ks[2], (L, IN_WIDTH), 0.02),
        "conv_dw_w": nrm(ks[3], (L, CONV_WIDTH, CONV_CHANNELS), CONV_WIDTH ** -0.5),
        "conv_dw_b": nrm(ks[4], (L, CONV_CHANNELS), 0.02),
        "conv_ln_g": 1.0 + nrm(ks[5], (L, CONV_CHANNELS), 0.02),
        "conv_ln_b": nrm(ks[6], (L, CONV_CHANNELS), 0.02),
        "w_conv_proj": nrm(ks[7], (L, CONV_CHANNELS, D_MODEL), CONV_CHANNELS ** -0.5),
        "b_conv_proj": nrm(ks[8], (L, D_MODEL), 0.02),
        "short_conv_w": nrm(ks[9], (L, SHORT_CONV, 2 * KEY_DIM + VALUE_DIM), SHORT_CONV ** -0.5),
        "a_log": jnp.log(jax.random.uniform(ks[11], (L, N_HEADS), f32, 1.0, 16.0)),
        "dt_bias": dt + jnp.log(-jnp.expm1(-dt)),
        "gdn_norm_w": 1.0 + nrm(ks[12], (L, HEAD_DIM_V), 0.02),
        "w_gdn_proj": nrm(ks[13], (L, VALUE_DIM, D_MODEL), VALUE_DIM ** -0.5),
        "w_out": nrm(ks[14], (L, D_MODEL, D_MODEL), DEEPNORM_BETA * D_MODEL ** -0.5),
        "ln1_g": 1.0 + nrm(ks[15], (L, D_MODEL), 0.02),
        "ln1_b": nrm(ks[16], (L, D_MODEL), 0.02),
        "w_ffn_in": nrm(ks[17], (L, D_MODEL, 2 * D_FF), D_MODEL ** -0.5),
        "w_ffn_out": nrm(ks[18], (L, D_FF, D_MODEL), DEEPNORM_BETA * D_FF ** -0.5),
        "ln2_g": 1.0 + nrm(ks[19], (L, D_MODEL), 0.02),
        "ln2_b": nrm(ks[20], (L, D_MODEL), 0.02),
    }


def _fwd_reference(x, w_in, b_in, conv_dw_w, conv_dw_b, conv_ln_g, conv_ln_b, w_conv_proj, b_conv_proj,
              short_conv_w, a_log, dt_bias, gdn_norm_w, w_gdn_proj, w_out, ln1_g, ln1_b,
              w_ffn_in, w_ffn_out, ln2_g, ln2_b):
    for l in range(DEPTH):
        mix = mixer(x, w_in[l], b_in[l], conv_dw_w[l], conv_dw_b[l], conv_ln_g[l], conv_ln_b[l],
                    w_conv_proj[l], b_conv_proj[l], short_conv_w[l], a_log[l], dt_bias[l],
                    gdn_norm_w[l], w_gdn_proj[l], w_out[l])
        x = layer_norm(DEEPNORM_ALPHA * x + mix, ln1_g[l], ln1_b[l])
        x = layer_norm(DEEPNORM_ALPHA * x + swiglu(x, w_ffn_in[l], w_ffn_out[l]), ln2_g[l], ln2_b[l])
    return x


import jax as _jax
import jax.numpy as _jnp

TWIN_FORMAT = 'train_step'
FWD_PARAMS = ['x', 'w_in', 'b_in', 'conv_dw_w', 'conv_dw_b', 'conv_ln_g', 'conv_ln_b', 'w_conv_proj', 'b_conv_proj', 'short_conv_w', 'a_log', 'dt_bias', 'gdn_norm_w', 'w_gdn_proj', 'w_out', 'ln1_g', 'ln1_b', 'w_ffn_in', 'w_ffn_out', 'ln2_g', 'ln2_b']
TWIN_WEIGHTS = ['w_in', 'b_in', 'conv_dw_w', 'conv_dw_b', 'conv_ln_g', 'conv_ln_b', 'w_conv_proj', 'b_conv_proj', 'short_conv_w', 'a_log', 'dt_bias', 'gdn_norm_w', 'w_gdn_proj', 'w_out', 'ln1_g', 'ln1_b', 'w_ffn_in', 'w_ffn_out', 'ln2_g', 'ln2_b']
TWIN_DIFF_INPUT = 'x'
TWIN_INPUTS = ['x', 'w_in', 'b_in', 'conv_dw_w', 'conv_dw_b', 'conv_ln_g', 'conv_ln_b', 'w_conv_proj', 'b_conv_proj', 'short_conv_w', 'a_log', 'dt_bias', 'gdn_norm_w', 'w_gdn_proj', 'w_out', 'ln1_g', 'ln1_b', 'w_ffn_in', 'w_ffn_out', 'ln2_g', 'ln2_b', 'loss_target', 'm_w_in', 'm_b_in', 'm_conv_dw_w', 'm_conv_dw_b', 'm_conv_ln_g', 'm_conv_ln_b', 'm_w_conv_proj', 'm_b_conv_proj', 'm_short_conv_w', 'm_a_log', 'm_dt_bias', 'm_gdn_norm_w', 'm_w_gdn_proj', 'm_w_out', 'm_ln1_g', 'm_ln1_b', 'm_w_ffn_in', 'm_w_ffn_out', 'm_ln2_g', 'm_ln2_b', 'v_w_in', 'v_b_in', 'v_conv_dw_w', 'v_conv_dw_b', 'v_conv_ln_g', 'v_conv_ln_b', 'v_w_conv_proj', 'v_b_conv_proj', 'v_short_conv_w', 'v_a_log', 'v_dt_bias', 'v_gdn_norm_w', 'v_w_gdn_proj', 'v_w_out', 'v_ln1_g', 'v_ln1_b', 'v_w_ffn_in', 'v_w_ffn_out', 'v_ln2_g', 'v_ln2_b']
TWIN_OUTPUTS = ['loss', 'grad_x', 'grad_w_in', 'grad_b_in', 'grad_conv_dw_w', 'grad_conv_dw_b', 'grad_conv_ln_g', 'grad_conv_ln_b', 'grad_w_conv_proj', 'grad_b_conv_proj', 'grad_short_conv_w', 'grad_a_log', 'grad_dt_bias', 'grad_gdn_norm_w', 'grad_w_gdn_proj', 'grad_w_out', 'grad_ln1_g', 'grad_ln1_b', 'grad_w_ffn_in', 'grad_w_ffn_out', 'grad_ln2_g', 'grad_ln2_b', 'delta_w_in', 'delta_b_in', 'delta_conv_dw_w', 'delta_conv_dw_b', 'delta_conv_ln_g', 'delta_conv_ln_b', 'delta_w_conv_proj', 'delta_b_conv_proj', 'delta_short_conv_w', 'delta_a_log', 'delta_dt_bias', 'delta_gdn_norm_w', 'delta_w_gdn_proj', 'delta_w_out', 'delta_ln1_g', 'delta_ln1_b', 'delta_w_ffn_in', 'delta_w_ffn_out', 'delta_ln2_g', 'delta_ln2_b', 'new_m_w_in', 'new_m_b_in', 'new_m_conv_dw_w', 'new_m_conv_dw_b', 'new_m_conv_ln_g', 'new_m_conv_ln_b', 'new_m_w_conv_proj', 'new_m_b_conv_proj', 'new_m_short_conv_w', 'new_m_a_log', 'new_m_dt_bias', 'new_m_gdn_norm_w', 'new_m_w_gdn_proj', 'new_m_w_out', 'new_m_ln1_g', 'new_m_ln1_b', 'new_m_w_ffn_in', 'new_m_w_ffn_out', 'new_m_ln2_g', 'new_m_ln2_b', 'new_v_w_in', 'new_v_b_in', 'new_v_conv_dw_w', 'new_v_conv_dw_b', 'new_v_conv_ln_g', 'new_v_conv_ln_b', 'new_v_w_conv_proj', 'new_v_b_conv_proj', 'new_v_short_conv_w', 'new_v_a_log', 'new_v_dt_bias', 'new_v_gdn_norm_w', 'new_v_w_gdn_proj', 'new_v_w_out', 'new_v_ln1_g', 'new_v_ln1_b', 'new_v_w_ffn_in', 'new_v_w_ffn_out', 'new_v_ln2_g', 'new_v_ln2_b']
TWIN_LEAF_KINDS = {'loss': 'loss', 'grad_x': 'grad_x', 'grad_w_in': 'grad_w', 'grad_b_in': 'grad_w', 'grad_conv_dw_w': 'grad_w', 'grad_conv_dw_b': 'grad_w', 'grad_conv_ln_g': 'grad_w', 'grad_conv_ln_b': 'grad_w', 'grad_w_conv_proj': 'grad_w', 'grad_b_conv_proj': 'grad_w', 'grad_short_conv_w': 'grad_w', 'grad_a_log': 'grad_w', 'grad_dt_bias': 'grad_w', 'grad_gdn_norm_w': 'grad_w', 'grad_w_gdn_proj': 'grad_w', 'grad_w_out': 'grad_w', 'grad_ln1_g': 'grad_w', 'grad_ln1_b': 'grad_w', 'grad_w_ffn_in': 'grad_w', 'grad_w_ffn_out': 'grad_w', 'grad_ln2_g': 'grad_w', 'grad_ln2_b': 'grad_w', 'delta_w_in': 'delta_w', 'delta_b_in': 'delta_w', 'delta_conv_dw_w': 'delta_w', 'delta_conv_dw_b': 'delta_w', 'delta_conv_ln_g': 'delta_w', 'delta_conv_ln_b': 'delta_w', 'delta_w_conv_proj': 'delta_w', 'delta_b_conv_proj': 'delta_w', 'delta_short_conv_w': 'delta_w', 'delta_a_log': 'delta_w', 'delta_dt_bias': 'delta_w', 'delta_gdn_norm_w': 'delta_w', 'delta_w_gdn_proj': 'delta_w', 'delta_w_out': 'delta_w', 'delta_ln1_g': 'delta_w', 'delta_ln1_b': 'delta_w', 'delta_w_ffn_in': 'delta_w', 'delta_w_ffn_out': 'delta_w', 'delta_ln2_g': 'delta_w', 'delta_ln2_b': 'delta_w', 'new_m_w_in': 'new_m', 'new_m_b_in': 'new_m', 'new_m_conv_dw_w': 'new_m', 'new_m_conv_dw_b': 'new_m', 'new_m_conv_ln_g': 'new_m', 'new_m_conv_ln_b': 'new_m', 'new_m_w_conv_proj': 'new_m', 'new_m_b_conv_proj': 'new_m', 'new_m_short_conv_w': 'new_m', 'new_m_a_log': 'new_m', 'new_m_dt_bias': 'new_m', 'new_m_gdn_norm_w': 'new_m', 'new_m_w_gdn_proj': 'new_m', 'new_m_w_out': 'new_m', 'new_m_ln1_g': 'new_m', 'new_m_ln1_b': 'new_m', 'new_m_w_ffn_in': 'new_m', 'new_m_w_ffn_out': 'new_m', 'new_m_ln2_g': 'new_m', 'new_m_ln2_b': 'new_m', 'new_v_w_in': 'new_v', 'new_v_b_in': 'new_v', 'new_v_conv_dw_w': 'new_v', 'new_v_conv_dw_b': 'new_v', 'new_v_conv_ln_g': 'new_v', 'new_v_conv_ln_b': 'new_v', 'new_v_w_conv_proj': 'new_v', 'new_v_b_conv_proj': 'new_v', 'new_v_short_conv_w': 'new_v', 'new_v_a_log': 'new_v', 'new_v_dt_bias': 'new_v', 'new_v_gdn_norm_w': 'new_v', 'new_v_w_gdn_proj': 'new_v', 'new_v_w_out': 'new_v', 'new_v_ln1_g': 'new_v', 'new_v_ln1_b': 'new_v', 'new_v_w_ffn_in': 'new_v', 'new_v_w_ffn_out': 'new_v', 'new_v_ln2_g': 'new_v', 'new_v_ln2_b': 'new_v'}


def _forward(args):
    return _fwd_reference(*[args[k] for k in FWD_PARAMS])


def _output_shape():
    def fwd():
        inp = _fwd_setup_inputs(0)
        return _fwd_reference(*[inp[k] for k in FWD_PARAMS])
    out = _jax.eval_shape(fwd)
    return out.shape, out.dtype

N_MICROBATCH = 1
ADAM_LR = 0.001
ADAM_B1 = 0.9
ADAM_B2 = 0.999
ADAM_EPS = 1e-08
ADAM_WD = 0.01
ADAM_STEP = 10
PER_EXAMPLE_BATCH_AXIS = {'x': 0, 'loss_target': 0}
SHARED_INPUTS = []
_WEIGHT_DTYPES = {'w_in': _jnp.float32, 'b_in': _jnp.float32, 'conv_dw_w': _jnp.float32, 'conv_dw_b': _jnp.float32, 'conv_ln_g': _jnp.float32, 'conv_ln_b': _jnp.float32, 'w_conv_proj': _jnp.float32, 'b_conv_proj': _jnp.float32, 'short_conv_w': _jnp.float32, 'a_log': _jnp.float32, 'dt_bias': _jnp.float32, 'gdn_norm_w': _jnp.float32, 'w_gdn_proj': _jnp.float32, 'w_out': _jnp.float32, 'ln1_g': _jnp.float32, 'ln1_b': _jnp.float32, 'w_ffn_in': _jnp.float32, 'w_ffn_out': _jnp.float32, 'ln2_g': _jnp.float32, 'ln2_b': _jnp.float32}
MOMENT_SCALE = {'w_in': 7.674565e-03, 'b_in': 1.391859e-02, 'conv_dw_w': 1.118983e-02, 'conv_dw_b': 4.911781e-02, 'conv_ln_g': 2.124245e-02, 'conv_ln_b': 3.002614e-02, 'w_conv_proj': 1.413946e-02, 'b_conv_proj': 6.128198e-02, 'short_conv_w': 8.300393e-03, 'a_log': 4.107130e-02, 'dt_bias': 3.901355e-02, 'gdn_norm_w': 4.227634e-02, 'w_gdn_proj': 1.163067e-02, 'w_out': 4.343789e-02, 'ln1_g': 1.048853e+00, 'ln1_b': 4.963523e-01, 'w_ffn_in': 1.166580e-02, 'w_ffn_out': 4.530396e-02, 'ln2_g': 1.606733e+01, 'ln2_b': 1.262606e+00}


def _to_microbatches(a, axis):
    t = _jnp.moveaxis(a, axis, 0)
    t = t.reshape((N_MICROBATCH, t.shape[0] // N_MICROBATCH) + t.shape[1:])
    return _jnp.moveaxis(t, 1, axis + 1)


def setup_inputs(seed: int = 0) -> dict:
    inp = _fwd_setup_inputs(seed)
    key = _jax.random.fold_in(_jax.random.key(seed), 7919)
    shape, _ = _output_shape()
    out = dict(inp)
    out["loss_target"] = _jax.random.normal(_jax.random.fold_in(key, 0), shape, _jnp.float32)
    for i, name in enumerate(TWIN_WEIGHTS):
        w = inp[name].astype(_jnp.float32)
        if MOMENT_SCALE is None:
            s = _jnp.sqrt(_jnp.mean(_jnp.square(w)) + 1e-30)
        else:
            s = MOMENT_SCALE[name]
        km, kv = _jax.random.split(_jax.random.fold_in(key, i + 1))
        out[name] = w
        out["m_" + name] = s * _jax.random.normal(km, w.shape, _jnp.float32)
        out["v_" + name] = (s * s) * _jax.random.uniform(kv, w.shape, _jnp.float32, 0.5, 1.5)
    if N_MICROBATCH > 1:
        for name, axis in PER_EXAMPLE_BATCH_AXIS.items():
            out[name] = _to_microbatches(out[name], axis)
    return {'x': out['x'], 'w_in': out['w_in'], 'b_in': out['b_in'], 'conv_dw_w': out['conv_dw_w'], 'conv_dw_b': out['conv_dw_b'], 'conv_ln_g': out['conv_ln_g'], 'conv_ln_b': out['conv_ln_b'], 'w_conv_proj': out['w_conv_proj'], 'b_conv_proj': out['b_conv_proj'], 'short_conv_w': out['short_conv_w'], 'a_log': out['a_log'], 'dt_bias': out['dt_bias'], 'gdn_norm_w': out['gdn_norm_w'], 'w_gdn_proj': out['w_gdn_proj'], 'w_out': out['w_out'], 'ln1_g': out['ln1_g'], 'ln1_b': out['ln1_b'], 'w_ffn_in': out['w_ffn_in'], 'w_ffn_out': out['w_ffn_out'], 'ln2_g': out['ln2_g'], 'ln2_b': out['ln2_b'], 'loss_target': out['loss_target'], 'm_w_in': out['m_w_in'], 'm_b_in': out['m_b_in'], 'm_conv_dw_w': out['m_conv_dw_w'], 'm_conv_dw_b': out['m_conv_dw_b'], 'm_conv_ln_g': out['m_conv_ln_g'], 'm_conv_ln_b': out['m_conv_ln_b'], 'm_w_conv_proj': out['m_w_conv_proj'], 'm_b_conv_proj': out['m_b_conv_proj'], 'm_short_conv_w': out['m_short_conv_w'], 'm_a_log': out['m_a_log'], 'm_dt_bias': out['m_dt_bias'], 'm_gdn_norm_w': out['m_gdn_norm_w'], 'm_w_gdn_proj': out['m_w_gdn_proj'], 'm_w_out': out['m_w_out'], 'm_ln1_g': out['m_ln1_g'], 'm_ln1_b': out['m_ln1_b'], 'm_w_ffn_in': out['m_w_ffn_in'], 'm_w_ffn_out': out['m_w_ffn_out'], 'm_ln2_g': out['m_ln2_g'], 'm_ln2_b': out['m_ln2_b'], 'v_w_in': out['v_w_in'], 'v_b_in': out['v_b_in'], 'v_conv_dw_w': out['v_conv_dw_w'], 'v_conv_dw_b': out['v_conv_dw_b'], 'v_conv_ln_g': out['v_conv_ln_g'], 'v_conv_ln_b': out['v_conv_ln_b'], 'v_w_conv_proj': out['v_w_conv_proj'], 'v_b_conv_proj': out['v_b_conv_proj'], 'v_short_conv_w': out['v_short_conv_w'], 'v_a_log': out['v_a_log'], 'v_dt_bias': out['v_dt_bias'], 'v_gdn_norm_w': out['v_gdn_norm_w'], 'v_w_gdn_proj': out['v_w_gdn_proj'], 'v_w_out': out['v_w_out'], 'v_ln1_g': out['v_ln1_g'], 'v_ln1_b': out['v_ln1_b'], 'v_w_ffn_in': out['v_w_ffn_in'], 'v_w_ffn_out': out['v_w_ffn_out'], 'v_ln2_g': out['v_ln2_g'], 'v_ln2_b': out['v_ln2_b']}


def _loss(weights, diff, rest, loss_target):
    with _jax.named_scope("forward"):
        args = {**rest, TWIN_DIFF_INPUT: diff, **{k: w.astype(_WEIGHT_DTYPES[k]) for k, w in weights.items()}}
        y = _forward(args)
    with _jax.named_scope("loss_head"):
        err = _jnp.square(y.astype(_jnp.float32) - loss_target)
        return 0.5 * _jnp.sum(_jnp.mean(err, axis=-1)) if err.ndim else 0.5 * err


def _adamw(w, g, m, v):
    m = ADAM_B1 * m + (1.0 - ADAM_B1) * g
    v = ADAM_B2 * v + (1.0 - ADAM_B2) * _jnp.square(g)
    m_hat = m / (1.0 - ADAM_B1 ** ADAM_STEP)
    v_hat = v / (1.0 - ADAM_B2 ** ADAM_STEP)
    delta = -ADAM_LR * (m_hat / (_jnp.sqrt(v_hat) + ADAM_EPS) + ADAM_WD * w)
    return delta, m, v


def reference(x, w_in, b_in, conv_dw_w, conv_dw_b, conv_ln_g, conv_ln_b, w_conv_proj, b_conv_proj, short_conv_w, a_log, dt_bias, gdn_norm_w, w_gdn_proj, w_out, ln1_g, ln1_b, w_ffn_in, w_ffn_out, ln2_g, ln2_b, loss_target, m_w_in, m_b_in, m_conv_dw_w, m_conv_dw_b, m_conv_ln_g, m_conv_ln_b, m_w_conv_proj, m_b_conv_proj, m_short_conv_w, m_a_log, m_dt_bias, m_gdn_norm_w, m_w_gdn_proj, m_w_out, m_ln1_g, m_ln1_b, m_w_ffn_in, m_w_ffn_out, m_ln2_g, m_ln2_b, v_w_in, v_b_in, v_conv_dw_w, v_conv_dw_b, v_conv_ln_g, v_conv_ln_b, v_w_conv_proj, v_b_conv_proj, v_short_conv_w, v_a_log, v_dt_bias, v_gdn_norm_w, v_w_gdn_proj, v_w_out, v_ln1_g, v_ln1_b, v_w_ffn_in, v_w_ffn_out, v_ln2_g, v_ln2_b):
    given = dict(x=x, w_in=w_in, b_in=b_in, conv_dw_w=conv_dw_w, conv_dw_b=conv_dw_b, conv_ln_g=conv_ln_g, conv_ln_b=conv_ln_b, w_conv_proj=w_conv_proj, b_conv_proj=b_conv_proj, short_conv_w=short_conv_w, a_log=a_log, dt_bias=dt_bias, gdn_norm_w=gdn_norm_w, w_gdn_proj=w_gdn_proj, w_out=w_out, ln1_g=ln1_g, ln1_b=ln1_b, w_ffn_in=w_ffn_in, w_ffn_out=w_ffn_out, ln2_g=ln2_g, ln2_b=ln2_b, loss_target=loss_target, m_w_in=m_w_in, m_b_in=m_b_in, m_conv_dw_w=m_conv_dw_w, m_conv_dw_b=m_conv_dw_b, m_conv_ln_g=m_conv_ln_g, m_conv_ln_b=m_conv_ln_b, m_w_conv_proj=m_w_conv_proj, m_b_conv_proj=m_b_conv_proj, m_short_conv_w=m_short_conv_w, m_a_log=m_a_log, m_dt_bias=m_dt_bias, m_gdn_norm_w=m_gdn_norm_w, m_w_gdn_proj=m_w_gdn_proj, m_w_out=m_w_out, m_ln1_g=m_ln1_g, m_ln1_b=m_ln1_b, m_w_ffn_in=m_w_ffn_in, m_w_ffn_out=m_w_ffn_out, m_ln2_g=m_ln2_g, m_ln2_b=m_ln2_b, v_w_in=v_w_in, v_b_in=v_b_in, v_conv_dw_w=v_conv_dw_w, v_conv_dw_b=v_conv_dw_b, v_conv_ln_g=v_conv_ln_g, v_conv_ln_b=v_conv_ln_b, v_w_conv_proj=v_w_conv_proj, v_b_conv_proj=v_b_conv_proj, v_short_conv_w=v_short_conv_w, v_a_log=v_a_log, v_dt_bias=v_dt_bias, v_gdn_norm_w=v_gdn_norm_w, v_w_gdn_proj=v_w_gdn_proj, v_w_out=v_w_out, v_ln1_g=v_ln1_g, v_ln1_b=v_ln1_b, v_w_ffn_in=v_w_ffn_in, v_w_ffn_out=v_w_ffn_out, v_ln2_g=v_ln2_g, v_ln2_b=v_ln2_b)
    weights = {n: given[n] for n in TWIN_WEIGHTS}
    shared = {n: given[n] for n in SHARED_INPUTS}
    per_example = {n: given[n] for n in ['x']}
    grad_fn = _jax.value_and_grad(_loss, argnums=(0, 1))

    def one_microbatch(ex, loss_target):
        ex = dict(ex)
        diff = ex.pop(TWIN_DIFF_INPUT)
        return grad_fn(weights, diff, {**shared, **ex}, loss_target)

    if N_MICROBATCH == 1:
        loss, (grad_w, grad_x) = one_microbatch(per_example, given["loss_target"])
    else:
        def body(carry, xs):
            loss_sum, grad_sum = carry
            l_k, (gw_k, gx_k) = one_microbatch(xs[0], xs[1])
            with _jax.named_scope("update"):
                return (loss_sum + l_k, _jax.tree.map(_jnp.add, grad_sum, gw_k)), gx_k

        init = (_jnp.zeros((), _jnp.float32), _jax.tree.map(_jnp.zeros_like, weights))
        (loss, grad_w), grad_x = _jax.lax.scan(body, init, (per_example, given["loss_target"]))
    with _jax.named_scope("update"):
        delta_w, new_m, new_v = {}, {}, {}
        for n in TWIN_WEIGHTS:
            delta_w[n], new_m[n], new_v[n] = _adamw(weights[n], grad_w[n], given["m_" + n], given["v_" + n])
    return (loss, grad_x, *[grad_w[n] for n in TWIN_WEIGHTS], *[delta_w[n] for n in TWIN_WEIGHTS],
            *[new_m[n] for n in TWIN_WEIGHTS], *[new_v[n] for n in TWIN_WEIGHTS])
```

```python
import functools

import jax
import jax.numpy as jnp
import numpy as np
from jax import lax
from jax.experimental import pallas as pl
from jax.experimental.pallas import tpu as pltpu

F32 = jnp.float32
BF16 = jnp.bfloat16
MESH = pl.DeviceIdType.MESH

LN_EPS = 1e-5
L2_EPS = 1e-6
CHUNK = 64
HEAD = 128
CONV_HALO = 32
SHORT_HALO = 8
ROW_BLOCK = 256
VMEM_LIMIT = 56 * 1024 * 1024
LANE = 128
FLAT_W = 1024

ADAM_LR, ADAM_B1, ADAM_B2, ADAM_EPS, ADAM_WD, ADAM_STEP = 0.001, 0.9, 0.999, 1e-08, 0.01, 10

NN = ((1,), (0,))
NT = ((1,), (1,))
TN = ((0,), (0,))


def _params(n_axes):
    return pltpu.CompilerParams(dimension_semantics=("arbitrary",) * n_axes, vmem_limit_bytes=VMEM_LIMIT)


def _tile(dim, pref, unit=LANE):
    if dim <= pref:
        return dim
    best = None
    for t in range(unit, pref + 1, unit):
        if dim % t == 0:
            best = t
    assert best is not None, (dim, pref, unit)
    return best


def _dotb(a, b, dims):
    return lax.dot_general(a.astype(BF16), b.astype(BF16), (dims, ((), ())), preferred_element_type=F32)


def _dotf(a, b, dims):
    return lax.dot_general(a, b, (dims, ((), ())), precision=lax.Precision.HIGHEST, preferred_element_type=F32)


def _sig(x):
    return jax.nn.sigmoid(x)


def _dsilu(x):
    s = _sig(x)
    return s * (1.0 + x * (1.0 - s))


def _softplus(x):
    return jnp.maximum(x, 0.0) + jnp.log(1.0 + jnp.exp(-jnp.abs(x)))


def _iota(shape, dim):
    return lax.broadcasted_iota(jnp.int32, shape, dim)


def _accum(ref, val, first):
    @pl.when(first)
    def _():
        ref[...] = val

    @pl.when(jnp.logical_not(first))
    def _():
        ref[...] += val


def _mm(name, a, b, mode, out_dtype=F32, bias=None, res=None, res_scale=1.0, tm=512, tn=1024, tk=1024):
    if mode == "nn":
        (m, k), (k2, n) = a.shape, b.shape
    elif mode == "tn":
        (k, m), (k2, n) = a.shape, b.shape
    else:
        (m, k), (n, k2) = a.shape, b.shape
    assert k == k2, (name, a.shape, b.shape)
    tm, tn, tk = _tile(m, tm), _tile(n, tn), _tile(k, tk)
    nk = k // tk
    dims = {"nn": NN, "tn": TN, "nt": NT}[mode]
    a_spec = {"nn": pl.BlockSpec((tm, tk), lambda i, j, kk: (i, kk)),
              "tn": pl.BlockSpec((tk, tm), lambda i, j, kk: (kk, i)),
              "nt": pl.BlockSpec((tm, tk), lambda i, j, kk: (i, kk))}[mode]
    b_spec = {"nn": pl.BlockSpec((tk, tn), lambda i, j, kk: (kk, j)),
              "tn": pl.BlockSpec((tk, tn), lambda i, j, kk: (kk, j)),
              "nt": pl.BlockSpec((tn, tk), lambda i, j, kk: (j, kk))}[mode]
    ins, in_specs = [a, b], [a_spec, b_spec]
    if bias is not None:
        ins.append(bias)
        in_specs.append(pl.BlockSpec((1, tn), lambda i, j, kk: (0, j)))
    if res is not None:
        ins.append(res)
        in_specs.append(pl.BlockSpec((tm, tn), lambda i, j, kk: (i, j)))
    has_bias, has_res = bias is not None, res is not None

    def body(*refs):
        a_ref, b_ref = refs[0], refs[1]
        pos = 2
        bias_ref = res_ref = None
        if has_bias:
            bias_ref = refs[pos]
            pos += 1
        if has_res:
            res_ref = refs[pos]
            pos += 1
        o_ref, acc_ref = refs[pos], refs[pos + 1]
        kk = pl.program_id(2)
        part = _dotb(a_ref[...], b_ref[...], dims)
        _accum(acc_ref, part, kk == 0)

        @pl.when(kk == nk - 1)
        def _():
            r = acc_ref[...]
            if has_bias:
                r = r + bias_ref[...]
            if has_res:
                r = r + res_scale * res_ref[...]
            o_ref[...] = r.astype(out_dtype)

    return pl.pallas_call(
        body, name=name, grid=(m // tm, n // tn, nk), in_specs=in_specs,
        out_specs=pl.BlockSpec((tm, tn), lambda i, j, kk: (i, j)),
        out_shape=jax.ShapeDtypeStruct((m, n), out_dtype),
        scratch_shapes=[pltpu.VMEM((tm, tn), F32)], compiler_params=_params(3),
    )(*ins)


def _ew(name, fn, grid, ins, in_specs, out_shapes, out_specs, scratch=()):
    n_in, n_out = len(ins), len(out_shapes)

    def body(*refs):
        fn(refs[:n_in], refs[n_in:n_in + n_out], refs[n_in + n_out:])

    out = pl.pallas_call(
        body, name=name, grid=grid, in_specs=list(in_specs), out_specs=list(out_specs),
        out_shape=list(out_shapes), scratch_shapes=list(scratch), compiler_params=_params(len(grid)),
    )(*ins)
    return out


def _sds(shape, dtype=F32):
    return jax.ShapeDtypeStruct(tuple(shape), dtype)


def _rows(bs, w, col=0):
    return pl.BlockSpec((bs, w), lambda i, col=col: (i, col))


def _par(r, w):
    return pl.BlockSpec((r, w), lambda i: (0, 0))


def _ln_stats(r):
    mu = jnp.mean(r, axis=-1, keepdims=True)
    xc = r - mu
    var = jnp.mean(xc * xc, axis=-1, keepdims=True)
    rstd = lax.rsqrt(var + LN_EPS)
    return xc * rstd, rstd


def _ln_back(xh, rstd, g, dy):
    dxh = dy * g
    m1 = jnp.mean(dxh, axis=-1, keepdims=True)
    m2 = jnp.mean(dxh * xh, axis=-1, keepdims=True)
    return rstd * (dxh - m1 - xh * m2)


def _colsum(v):
    return jnp.sum(v, axis=0, keepdims=True)


def _ln_res_fwd(x, sub, g, b, alpha):
    s, d = x.shape
    bs = min(ROW_BLOCK, s)

    def fn(i_refs, o_refs, _):
        x_ref, s_ref, g_ref, b_ref = i_refs
        xh, _r = _ln_stats(alpha * x_ref[...] + s_ref[...])
        o_refs[0][...] = xh * g_ref[...] + b_ref[...]

    return _ew("ln_res_fwd", fn, (s // bs,), [x, sub, g, b], [_rows(bs, d), _rows(bs, d), _par(1, d), _par(1, d)],
               [_sds((s, d))], [_rows(bs, d)])[0]


def _ln_res_bwd(x, sub, g, dy, alpha):
    s, d = x.shape
    bs = min(ROW_BLOCK, s)

    def fn(i_refs, o_refs, _):
        x_ref, s_ref, g_ref, dy_ref = i_refs
        dr_ref, dg_ref, db_ref = o_refs
        first = pl.program_id(0) == 0
        xh, rstd = _ln_stats(alpha * x_ref[...] + s_ref[...])
        dy_v = dy_ref[...]
        dr_ref[...] = _ln_back(xh, rstd, g_ref[...], dy_v)
        _accum(dg_ref, _colsum(dy_v * xh), first)
        _accum(db_ref, _colsum(dy_v), first)

    return _ew("ln_res_bwd", fn, (s // bs,), [x, sub, g, dy], [_rows(bs, d), _rows(bs, d), _par(1, d), _rows(bs, d)],
               [_sds((s, d)), _sds((1, d)), _sds((1, d))], [_rows(bs, d), _par(1, d), _par(1, d)])


def _loss_fwd_bwd(y, t):
    s, d = y.shape
    bs = min(ROW_BLOCK, s)

    def fn(i_refs, o_refs, _):
        err = i_refs[0][...] - i_refs[1][...]
        o_refs[1][...] = err * (1.0 / d)
        tot = jnp.sum(jnp.sum(err * err, axis=1, keepdims=True), axis=0, keepdims=True) * (1.0 / d)
        _accum(o_refs[0], jnp.broadcast_to(tot, (8, LANE)), pl.program_id(0) == 0)

    return _ew("loss", fn, (s // bs,), [y, t], [_rows(bs, d), _rows(bs, d)],
               [_sds((8, LANE)), _sds((s, d))], [_par(8, LANE), _rows(bs, d)])


def _conv_fwd(z, cw, cb, g, b, d):
    s = z.shape[0]
    kw = cw.shape[0]
    bs = min(ROW_BLOCK, s)
    hb = CONV_HALO
    r = bs // hb

    def prev(col):
        return pl.BlockSpec((hb, d), lambda i: (jnp.maximum(i * r - 1, 0), col))

    def fn(i_refs, o_refs, scr):
        a_ref, b_ref, ap_ref, bp_ref, w_ref, cb_ref, g_ref, be_ref = i_refs
        ext = scr[0]
        i = pl.program_id(0)
        ext[pl.ds(0, hb), :] = jnp.where(i > 0, ap_ref[...] * _sig(bp_ref[...]), 0.0)
        ext[pl.ds(hb, bs), :] = a_ref[...] * _sig(b_ref[...])
        acc = jnp.zeros((bs, d), F32) + cb_ref[...]
        for j in range(kw):
            acc = acc + w_ref[pl.ds(j, 1), :] * ext[pl.ds(hb - (kw - 1) + j, bs), :]
        xh, _r = _ln_stats(acc)
        c2 = xh * g_ref[...] + be_ref[...]
        o_refs[0][...] = (c2 * _sig(c2)).astype(BF16)

    return _ew("conv_fwd", fn, (s // bs,), [z, z, z, z, cw, cb, g, b],
               [_rows(bs, d, 0), _rows(bs, d, 1), prev(0), prev(1), _par(kw, d), _par(1, d), _par(1, d), _par(1, d)],
               [_sds((s, d), BF16)], [_rows(bs, d)], scratch=[pltpu.VMEM((hb + bs, d), F32)])[0]


def _conv_bwd(z, dc3, cw, cb, g, b, d):
    s = z.shape[0]
    kw = cw.shape[0]
    bs = min(ROW_BLOCK, s)
    hb = CONV_HALO
    r = bs // hb
    nrow = s // bs
    last_h = s // hb - 1

    def prev(col):
        return pl.BlockSpec((hb, d), lambda i: (jnp.maximum(i * r - 1, 0), col))

    def nxt(col):
        return pl.BlockSpec((hb, d), lambda i: (jnp.minimum((i + 1) * r, last_h), col))

    def fn(i_refs, o_refs, scr):
        a_ref, b_ref, ap_ref, bp_ref, an_ref, bn_ref, d_ref, dn_ref, w_ref, cb_ref, g_ref, be_ref = i_refs
        da_ref, db_ref, dw_ref, dcb_ref, dg_ref, dbe_ref = o_refs
        ext, extd = scr
        i = pl.program_id(0)
        first = i == 0
        more = i < nrow - 1
        a_v, b_v = a_ref[...], b_ref[...]
        sb = _sig(b_v)
        ext[pl.ds(0, hb), :] = jnp.where(i > 0, ap_ref[...] * _sig(bp_ref[...]), 0.0)
        ext[pl.ds(hb, bs), :] = a_v * sb
        ext[pl.ds(hb + bs, hb), :] = jnp.where(more, an_ref[...] * _sig(bn_ref[...]), 0.0)
        n1 = bs + hb
        acc = jnp.zeros((n1, d), F32) + cb_ref[...]
        for j in range(kw):
            acc = acc + w_ref[pl.ds(j, 1), :] * ext[pl.ds(hb - (kw - 1) + j, n1), :]
        xh, rstd = _ln_stats(acc)
        c2 = xh * g_ref[...] + be_ref[...]
        dc3_e = jnp.concatenate([d_ref[...], jnp.where(more, dn_ref[...], 0.0)], axis=0)
        dc2 = dc3_e * _dsilu(c2)
        dc1 = _ln_back(xh, rstd, g_ref[...], dc2)
        extd[...] = dc1
        _accum(dg_ref, _colsum((dc2 * xh)[:bs]), first)
        _accum(dbe_ref, _colsum(dc2[:bs]), first)
        _accum(dcb_ref, _colsum(dc1[:bs]), first)

        @pl.when(first)
        def _():
            dw_ref[...] = jnp.zeros_like(dw_ref)

        dc1_own = extd[pl.ds(0, bs), :]
        dc0 = jnp.zeros((bs, d), F32)
        for j in range(kw):
            wj = w_ref[pl.ds(j, 1), :]
            dc0 = dc0 + wj * extd[pl.ds(kw - 1 - j, bs), :]
            dw_ref[pl.ds(j, 1), :] += _colsum(dc1_own * ext[pl.ds(hb - (kw - 1) + j, bs), :])
        da_ref[...] = (dc0 * sb).astype(BF16)
        db_ref[...] = (dc0 * a_v * sb * (1.0 - sb)).astype(BF16)

    return _ew("conv_bwd", fn, (nrow,), [z, z, z, z, z, z, dc3, dc3, cw, cb, g, b],
               [_rows(bs, d, 0), _rows(bs, d, 1), prev(0), prev(1), nxt(0), nxt(1), _rows(bs, d), nxt(0),
                _par(kw, d), _par(1, d), _par(1, d), _par(1, d)],
               [_sds((s, d), BF16), _sds((s, d), BF16), _sds((kw, d)), _sds((1, d)), _sds((1, d)), _sds((1, d))],
               [_rows(bs, d), _rows(bs, d), _par(kw, d), _par(1, d), _par(1, d), _par(1, d)],
               scratch=[pltpu.VMEM((bs + 2 * hb, d), F32), pltpu.VMEM((bs + hb, d), F32)])


def _sconv_fwd(z, sw, d, col0):
    s = z.shape[0]
    kw = sw.shape[0]
    bs = min(ROW_BLOCK, s)
    hb = SHORT_HALO
    r = bs // hb

    def fn(i_refs, o_refs, scr):
        x_ref, xp_ref, w_ref = i_refs
        ext = scr[0]
        i = pl.program_id(1)
        ext[pl.ds(0, hb), :] = jnp.where(i > 0, xp_ref[...], 0.0)
        ext[pl.ds(hb, bs), :] = x_ref[...]
        acc = jnp.zeros((bs, d), F32)
        for j in range(kw):
            acc = acc + w_ref[pl.ds(j, 1), :] * ext[pl.ds(hb - (kw - 1) + j, bs), :]
        o_refs[0][0] = acc * _sig(acc)

    return _ew("sconv_fwd", fn, (3, s // bs), [z, z, sw],
               [pl.BlockSpec((bs, d), lambda sg, i: (i, col0 + sg)),
                pl.BlockSpec((hb, d), lambda sg, i: (jnp.maximum(i * r - 1, 0), col0 + sg)),
                pl.BlockSpec((kw, d), lambda sg, i: (0, sg))],
               [_sds((3, s, d))], [pl.BlockSpec((1, bs, d), lambda sg, i: (sg, i, 0))],
               scratch=[pltpu.VMEM((hb + bs, d), F32)])[0]


def _sconv_bwd(z, dy, sw, d, col0):
    s = z.shape[0]
    kw = sw.shape[0]
    bs = min(ROW_BLOCK, s)
    hb = SHORT_HALO
    r = bs // hb
    nrow = s // bs
    last_h = s // hb - 1

    def fn(i_refs, o_refs, scr):
        x_ref, xp_ref, xn_ref, d_ref, dn_ref, w_ref = i_refs
        dx_ref, dw_ref = o_refs
        ext, extd = scr
        i = pl.program_id(1)
        more = i < nrow - 1
        ext[pl.ds(0, hb), :] = jnp.where(i > 0, xp_ref[...], 0.0)
        ext[pl.ds(hb, bs), :] = x_ref[...]
        ext[pl.ds(hb + bs, hb), :] = jnp.where(more, xn_ref[...], 0.0)
        n1 = bs + hb
        pre = jnp.zeros((n1, d), F32)
        for j in range(kw):
            pre = pre + w_ref[pl.ds(j, 1), :] * ext[pl.ds(hb - (kw - 1) + j, n1), :]
        dy_e = jnp.concatenate([d_ref[0], jnp.where(more, dn_ref[0], 0.0)], axis=0)
        extd[...] = dy_e * _dsilu(pre)

        @pl.when(i == 0)
        def _():
            dw_ref[...] = jnp.zeros_like(dw_ref)

        dp_own = extd[pl.ds(0, bs), :]
        dx = jnp.zeros((bs, d), F32)
        for j in range(kw):
            dx = dx + w_ref[pl.ds(j, 1), :] * extd[pl.ds(kw - 1 - j, bs), :]
            dw_ref[pl.ds(j, 1), :] += _colsum(dp_own * ext[pl.ds(hb - (kw - 1) + j, bs), :])
        dx_ref[0] = dx.astype(BF16)

    return _ew("sconv_bwd", fn, (3, nrow), [z, z, z, dy, dy, sw],
               [pl.BlockSpec((bs, d), lambda sg, i: (i, col0 + sg)),
                pl.BlockSpec((hb, d), lambda sg, i: (jnp.maximum(i * r - 1, 0), col0 + sg)),
                pl.BlockSpec((hb, d), lambda sg, i: (jnp.minimum((i + 1) * r, last_h), col0 + sg)),
                pl.BlockSpec((1, bs, d), lambda sg, i: (sg, i, 0)),
                pl.BlockSpec((1, hb, d), lambda sg, i: (sg, jnp.minimum((i + 1) * r, last_h), 0)),
                pl.BlockSpec((kw, d), lambda sg, i: (0, sg))],
               [_sds((3, s, d), BF16), _sds((kw, 3 * d))],
               [pl.BlockSpec((1, bs, d), lambda sg, i: (sg, i, 0)), pl.BlockSpec((kw, d), lambda sg, i: (0, sg))],
               scratch=[pltpu.VMEM((bs + 2 * hb, d), F32), pltpu.VMEM((bs + hb, d), F32)])


def _gdn_pre(qkv_ref, ba_ref, bat_ref, hp_ref):
    h = pl.program_id(0)
    c = CHUNK
    qr, kr, v = qkv_ref[0], qkv_ref[1], qkv_ref[2]
    ba = ba_ref[...]
    bat = bat_ref[0]
    nh = bat.shape[0] // 2
    lane = _iota(ba.shape, 1)
    sub = _iota(bat.shape, 0)
    braw_c = jnp.sum(jnp.where(lane == h, ba, 0.0), axis=1, keepdims=True)
    araw_c = jnp.sum(jnp.where(lane == h + nh, ba, 0.0), axis=1, keepdims=True)
    braw_r = jnp.sum(jnp.where(sub == h, bat, 0.0), axis=0, keepdims=True)
    araw_r = jnp.sum(jnp.where(sub == h + nh, bat, 0.0), axis=0, keepdims=True)
    alog = jnp.max(hp_ref[0, pl.ds(0, 1), :], axis=1, keepdims=True)
    dtb = jnp.max(hp_ref[0, pl.ds(1, 1), :], axis=1, keepdims=True)
    nega = -jnp.exp(alog)
    beta_c, beta_r = _sig(braw_c), _sig(braw_r)
    la_c = nega * _softplus(araw_c + dtb)
    la_r = nega * _softplus(araw_r + dtb)
    i = _iota((c, c), 0)
    j = _iota((c, c), 1)
    g_c = jnp.sum(jnp.where(j <= i, la_r, 0.0), axis=1, keepdims=True)
    g_r = jnp.sum(jnp.where(i <= j, la_c, 0.0), axis=0, keepdims=True)
    g_last = jnp.sum(la_c, axis=0, keepdims=True)
    low = i >= j
    dec = jnp.where(low, jnp.exp(jnp.where(low, g_c - g_r, 0.0)), 0.0)
    rq = lax.rsqrt(jnp.sum(qr * qr, axis=1, keepdims=True) + L2_EPS)
    rk = lax.rsqrt(jnp.sum(kr * kr, axis=1, keepdims=True) + L2_EPS)
    q = qr * (rq * HEAD ** -0.5)
    k = kr * rk
    kb = k * beta_c
    lmat = jnp.where(i > j, _dotb(kb, k, NT) * dec, 0.0)
    eye = jnp.where(i == j, 1.0, 0.0)
    tinv = eye - lmat
    pw = lmat
    for _ in range(int(np.log2(c)) - 1):
        pw = _dotf(pw, pw, NN)
        tinv = _dotf(tinv, eye + pw, NN)
    eg_c = jnp.exp(g_c)
    rhs_w = kb * eg_c
    sol = _dotf(tinv, jnp.concatenate([v * beta_c, rhs_w], axis=1), NN)
    attn = jnp.where(low, _dotb(q, k, NT) * dec, 0.0)
    ekd = jnp.exp(g_last - g_c)
    return dict(qr=qr, kr=kr, v=v, rq=rq, rk=rk, q=q, k=k, kb=kb, beta_c=beta_c, beta_r=beta_r, la_r=la_r,
                araw_r=araw_r, dtb=dtb, nega=nega, g_c=g_c, g_last=g_last, dec=dec, lmat=lmat, tinv=tinv,
                eg_c=eg_c, rhs_w=rhs_w, sol=sol, u=sol[:, :HEAD], w=sol[:, HEAD:], attn=attn, q_dec=q * eg_c,
                ekd=ekd, k_dec=k * ekd, i=i, j=j, low=low)


def _gdn_fwd(qkv, ba, bat, hp):
    _, s, d = qkv.shape
    nh, n, c = d // HEAD, s // CHUNK, CHUNK

    def fn(i_refs, o_refs, scr):
        o_ref, st_ref = o_refs
        s_scr = scr[0]

        @pl.when(pl.program_id(1) == 0)
        def _():
            s_scr[...] = jnp.zeros_like(s_scr)

        p = _gdn_pre(*i_refs)
        st = s_scr[...]
        st_ref[0, 0] = st
        vn = p["u"] - _dotb(p["w"], st, NN)
        o_ref[...] = _dotb(p["q_dec"], st, NN) + _dotb(p["attn"], vn, NN)
        s_scr[...] = st * jnp.exp(p["g_last"]) + _dotb(p["k_dec"], vn, TN)

    return _ew("gdn_fwd", fn, (nh, n), [qkv, ba, bat, hp],
               [pl.BlockSpec((3, c, HEAD), lambda h, t: (0, t, h)), pl.BlockSpec((c, LANE), lambda h, t: (t, 0)),
                pl.BlockSpec((1, bat.shape[1], c), lambda h, t: (t, 0, 0)), pl.BlockSpec((1, 8, LANE), lambda h, t: (h, 0, 0))],
               [_sds((s, d)), _sds((nh, n, HEAD, HEAD))],
               [pl.BlockSpec((c, HEAD), lambda h, t: (t, h)), pl.BlockSpec((1, 1, HEAD, HEAD), lambda h, t: (h, t, 0, 0))],
               scratch=[pltpu.VMEM((HEAD, HEAD), F32)])


def _gdn_bwd(qkv, ba, bat, hp, states, do):
    _, s, d = qkv.shape
    nh, n, c = d // HEAD, s // CHUNK, CHUNK

    def fn(i_refs, o_refs, scr):
        qkv_ref, ba_ref, bat_ref, hp_ref, st_ref, do_ref = i_refs
        dqkv_ref, dbat_ref, dhp_ref = o_refs
        ds_scr = scr[0]
        first = pl.program_id(1) == 0

        @pl.when(first)
        def _():
            ds_scr[...] = jnp.zeros_like(ds_scr)

        p = _gdn_pre(qkv_ref, ba_ref, bat_ref, hp_ref)
        i, j, low = p["i"], p["j"], p["low"]
        st = st_ref[0, 0]
        do_v = do_ref[...]
        ds1 = ds_scr[...]
        u, w, sol, attn, dec = p["u"], p["w"], p["sol"], p["attn"], p["dec"]
        q, k, kb, v = p["q"], p["k"], p["kb"], p["v"]
        eg_c, ekd, beta_c = p["eg_c"], p["ekd"], p["beta_c"]
        egl = jnp.exp(p["g_last"])
        vn = u - _dotb(w, st, NN)
        dvn = _dotb(attn, do_v, TN) + _dotb(p["k_dec"], ds1, NN)
        dattn = jnp.where(low, _dotb(do_v, vn, NT), 0.0)
        dqd = _dotb(do_v, st, NT)
        dkd = _dotb(vn, ds1, NT)
        ds_scr[...] = _dotb(p["q_dec"], do_v, TN) + egl * ds1 - _dotb(w, dvn, TN)
        dgl = jnp.sum(jnp.sum(st * ds1, axis=1, keepdims=True), axis=0, keepdims=True) * egl
        dw = -_dotb(dvn, st, NT)
        drhs = _dotf(p["tinv"], jnp.concatenate([dvn, dw], axis=1), TN)
        da = -jnp.where(i > j, _dotb(drhs, sol, NT), 0.0)
        mm = da * p["lmat"] + dattn * attn
        dg_c = jnp.sum(mm, axis=1, keepdims=True)
        dg_r = -jnp.sum(mm, axis=0, keepdims=True)
        dkk = da * dec
        dqk = dattn * dec
        dkb = _dotb(dkk, k, NN)
        dk = _dotb(dkk, kb, TN) + _dotb(dqk, q, TN)
        dq = _dotb(dqk, k, NN) + dqd * eg_c
        dg_c = dg_c + jnp.sum(dqd * p["q_dec"], axis=1, keepdims=True)
        dk = dk + dkd * ekd
        t = jnp.sum(dkd * p["k_dec"], axis=1, keepdims=True)
        dgl = dgl + jnp.sum(t, axis=0, keepdims=True)
        dg_c = dg_c - t
        drhs_u, drhs_w = drhs[:, :HEAD], drhs[:, HEAD:]
        dkb = dkb + drhs_w * eg_c
        dg_c = dg_c + jnp.sum(drhs_w * p["rhs_w"], axis=1, keepdims=True)
        dqkv_ref[2] = drhs_u * beta_c
        dbeta_c = jnp.sum(drhs_u * v, axis=1, keepdims=True) + jnp.sum(dkb * k, axis=1, keepdims=True)
        dk = dk + dkb * beta_c
        diag = i == j
        dg = dg_c + jnp.sum(jnp.where(diag, dg_r, 0.0), axis=1, keepdims=True)
        dla_r = jnp.sum(jnp.where(low, dg, 0.0), axis=0, keepdims=True) + dgl
        dbeta_r = jnp.sum(jnp.where(diag, dbeta_c, 0.0), axis=0, keepdims=True)
        beta_r = p["beta_r"]
        dbraw_r = dbeta_r * beta_r * (1.0 - beta_r)
        daraw_r = dla_r * p["nega"] * _sig(p["araw_r"] + p["dtb"])
        dalog = jnp.sum(dla_r * p["la_r"], axis=1, keepdims=True)
        ddtb = jnp.sum(daraw_r, axis=1, keepdims=True)
        row8 = _iota((8, c), 0)
        dbat_ref[0, 0] = jnp.where(row8 == 0, dbraw_r, jnp.where(row8 == 1, daraw_r, 0.0))
        rowp = _iota((8, LANE), 0)
        _accum(dhp_ref, jnp.where(rowp == 0, dalog, jnp.where(rowp == 1, ddtb, 0.0))[None], first)
        qr, kr, rq, rk = p["qr"], p["kr"], p["rq"], p["rk"]
        sc = HEAD ** -0.5
        dqkv_ref[0] = sc * (rq * dq - qr * (rq * rq * rq * jnp.sum(dq * qr, axis=1, keepdims=True)))
        dqkv_ref[1] = rk * dk - kr * (rk * rk * rk * jnp.sum(dk * kr, axis=1, keepdims=True))

    rev = n - 1
    return _ew("gdn_bwd", fn, (nh, n), [qkv, ba, bat, hp, states, do],
               [pl.BlockSpec((3, c, HEAD), lambda h, t: (0, rev - t, h)), pl.BlockSpec((c, LANE), lambda h, t: (rev - t, 0)),
                pl.BlockSpec((1, bat.shape[1], c), lambda h, t: (rev - t, 0, 0)), pl.BlockSpec((1, 8, LANE), lambda h, t: (h, 0, 0)),
                pl.BlockSpec((1, 1, HEAD, HEAD), lambda h, t: (h, rev - t, 0, 0)), pl.BlockSpec((c, HEAD), lambda h, t: (rev - t, h))],
               [_sds((3, s, d)), _sds((nh, n, 8, c)), _sds((nh, 8, LANE))],
               [pl.BlockSpec((3, c, HEAD), lambda h, t: (0, rev - t, h)), pl.BlockSpec((1, 1, 8, c), lambda h, t: (h, rev - t, 0, 0)),
                pl.BlockSpec((1, 8, LANE), lambda h, t: (h, 0, 0))],
               scratch=[pltpu.VMEM((HEAD, HEAD), F32)])


def _gdn_post_fwd(o, z, gw, d, zcol):
    s = o.shape[0]
    bs = min(ROW_BLOCK, s)

    def fn(i_refs, o_refs, _):
        o_ref, z_ref, w_ref = i_refs
        wv = w_ref[...]
        for h in range(d // HEAD):
            sl = pl.ds(h * HEAD, HEAD)
            oh, zg = o_ref[:, sl], z_ref[:, sl]
            r = lax.rsqrt(jnp.mean(oh * oh, axis=1, keepdims=True) + L2_EPS)
            o_refs[0][:, sl] = (oh * r * wv * (zg * _sig(zg))).astype(BF16)

    return _ew("gdn_post_fwd", fn, (s // bs,), [o, z, gw], [_rows(bs, d), _rows(bs, d, zcol), _par(1, HEAD)],
               [_sds((s, d), BF16)], [_rows(bs, d)])[0]


def _gdn_post_bwd(dog, o, z, gw, d, zcol):
    s = o.shape[0]
    bs = min(ROW_BLOCK, s)

    def fn(i_refs, o_refs, _):
        g_ref, o_ref, z_ref, w_ref = i_refs
        do_ref, dz_ref, dw_ref = o_refs
        wv = w_ref[...]
        dwacc = jnp.zeros((1, HEAD), F32)
        for h in range(d // HEAD):
            sl = pl.ds(h * HEAD, HEAD)
            oh, zg, gv = o_ref[:, sl], z_ref[:, sl], g_ref[:, sl]
            r = lax.rsqrt(jnp.mean(oh * oh, axis=1, keepdims=True) + L2_EPS)
            on = oh * r
            sil = zg * _sig(zg)
            dz_ref[:, sl] = (gv * on * wv * _dsilu(zg)).astype(BF16)
            dwacc = dwacc + _colsum(gv * on * sil)
            don = gv * wv * sil
            do_ref[:, sl] = r * (don - on * jnp.mean(don * on, axis=1, keepdims=True))
        _accum(dw_ref, dwacc, pl.program_id(0) == 0)

    return _ew("gdn_post_bwd", fn, (s // bs,), [dog, o, z, gw],
               [_rows(bs, d), _rows(bs, d), _rows(bs, d, zcol), _par(1, HEAD)],
               [_sds((s, d)), _sds((s, d), BF16), _sds((1, HEAD))], [_rows(bs, d), _rows(bs, d), _par(1, HEAD)])


def _merge_fwd(z, yc, yg, d, col_a, col_b):
    s = z.shape[0]
    bs = min(ROW_BLOCK, s)

    def fn(i_refs, o_refs, _):
        ga, gb, yc_ref, yg_ref = i_refs
        o_refs[0][...] = (_sig(ga[...]) * yc_ref[...] + _sig(gb[...]) * yg_ref[...]).astype(BF16)

    return _ew("merge_fwd", fn, (s // bs,), [z, z, yc, yg],
               [_rows(bs, d, col_a), _rows(bs, d, col_b), _rows(bs, d), _rows(bs, d)], [_sds((s, d), BF16)], [_rows(bs, d)])[0]


def _merge_bwd(dm, z, yc, yg, d, col_a, col_b):
    s = z.shape[0]
    bs = min(ROW_BLOCK, s)

    def fn(i_refs, o_refs, _):
        dm_ref, ga, gb, yc_ref, yg_ref = i_refs
        dyc_ref, dyg_ref, dga_ref, dgb_ref, dbc_ref = o_refs
        dmv = dm_ref[...]
        sa, sb = _sig(ga[...]), _sig(gb[...])
        dyc = dmv * sa
        dyc_ref[...] = dyc.astype(BF16)
        dyg_ref[...] = (dmv * sb).astype(BF16)
        dga_ref[...] = (dmv * yc_ref[...] * sa * (1.0 - sa)).astype(BF16)
        dgb_ref[...] = (dmv * yg_ref[...] * sb * (1.0 - sb)).astype(BF16)
        _accum(dbc_ref, _colsum(dyc), pl.program_id(0) == 0)

    return _ew("merge_bwd", fn, (s // bs,), [dm, z, z, yc, yg],
               [_rows(bs, d), _rows(bs, d, col_a), _rows(bs, d, col_b), _rows(bs, d), _rows(bs, d)],
               [_sds((s, d), BF16)] * 4 + [_sds((1, d))], [_rows(bs, d)] * 4 + [_par(1, d)])


def _swiglu_fwd(hg, hu):
    s, f = hg.shape
    bs = min(ROW_BLOCK, s)
    cw = _tile(f, 2048)

    def fn(i_refs, o_refs, _):
        g = i_refs[0][...]
        o_refs[0][...] = (g * _sig(g) * i_refs[1][...]).astype(BF16)

    spec = pl.BlockSpec((bs, cw), lambda i, j: (i, j))
    return _ew("swiglu_fwd", fn, (s // bs, f // cw), [hg, hu], [spec, spec], [_sds((s, f), BF16)], [spec])[0]


def _swiglu_bwd(hg, hu, df):
    s, f = hg.shape
    bs = min(ROW_BLOCK, s)
    cw = _tile(f, 2048)

    def fn(i_refs, o_refs, _):
        g, u, dfv = i_refs[0][...], i_refs[1][...], i_refs[2][...]
        o_refs[0][...] = (dfv * u * _dsilu(g)).astype(BF16)
        o_refs[1][...] = (dfv * g * _sig(g)).astype(BF16)

    spec = pl.BlockSpec((bs, cw), lambda i, j: (i, j))
    return _ew("swiglu_bwd", fn, (s // bs, f // cw), [hg, hu, df], [spec] * 3, [_sds((s, f), BF16)] * 2, [spec] * 2)


def _col_sums(name, a):
    s, n = a.shape
    bs = min(ROW_BLOCK, s)
    cw = _tile(n, 2048)

    def fn(i_refs, o_refs, _):
        _accum(o_refs[0], _colsum(i_refs[0][...].astype(F32)), pl.program_id(1) == 0)

    return _ew(name, fn, (n // cw, s // bs), [a], [pl.BlockSpec((bs, cw), lambda j, i: (i, j))],
               [_sds((1, n))], [pl.BlockSpec((1, cw), lambda j, i: (0, j))])[0]


def _adamw(name, w, g, m, v):
    r, c = w.shape
    br = r
    if r * c * 4 > (1 << 20):
        cands = [t for t in range(8, r, 8) if r % t == 0 and t * c * 4 <= (1 << 20)]
        br = max(cands) if cands else r
    c1 = 1.0 - ADAM_B1 ** ADAM_STEP
    c2 = 1.0 - ADAM_B2 ** ADAM_STEP

    def fn(i_refs, o_refs, _):
        wv, gv, mv, vv = (x[...] for x in i_refs)
        m2 = ADAM_B1 * mv + (1.0 - ADAM_B1) * gv
        v2 = ADAM_B2 * vv + (1.0 - ADAM_B2) * (gv * gv)
        o_refs[0][...] = -ADAM_LR * ((m2 / c1) / (jnp.sqrt(v2 / c2) + ADAM_EPS) + ADAM_WD * wv)
        o_refs[1][...] = m2
        o_refs[2][...] = v2

    spec = pl.BlockSpec((br, c), lambda i: (i, 0))
    return _ew(name, fn, (r // br,), [w, g, m, v], [spec] * 4, [_sds((r, c))] * 3, [spec] * 3)


ANY = pl.BlockSpec(memory_space=pl.ANY)


def _place():
    return lax.axis_index("x"), lax.axis_index("y"), lax.axis_index("c")


def _gather_chips(flat):
    r, w = flat.shape

    def body(x_ref, o_ref, ssem, rsem, lsem):
        x, y, c = _place()
        me = 2 * x + y
        chips = [(1 - x, y), (x, 1 - y), (1 - x, 1 - y)]
        local = pltpu.make_async_copy(x_ref, o_ref.at[me], lsem)
        local.start()
        sends = []
        for j, (px, py) in enumerate(chips):
            cp = pltpu.make_async_remote_copy(src_ref=x_ref, dst_ref=o_ref.at[me], send_sem=ssem.at[j], recv_sem=rsem.at[j],
                                              device_id=(px, py, c), device_id_type=MESH)
            cp.start()
            sends.append(cp)
        for j, (px, py) in enumerate(chips):
            pltpu.make_async_remote_copy(src_ref=x_ref, dst_ref=o_ref.at[2 * px + py], send_sem=ssem.at[j], recv_sem=rsem.at[j],
                                         device_id=(px, py, c), device_id_type=MESH).wait_recv()
        for cp in sends:
            cp.wait_send()
        local.wait()

    return pl.pallas_call(
        body, name="gather_chips", in_specs=[ANY], out_specs=ANY, out_shape=_sds((4, r, w), flat.dtype),
        scratch_shapes=[pltpu.SemaphoreType.DMA((3,)), pltpu.SemaphoreType.DMA((3,)), pltpu.SemaphoreType.DMA(())],
    )(flat)


def _gather_all(flat):
    r, w = flat.shape
    masks = [(mx, my, mc) for mx in (0, 1) for my in (0, 1) for mc in (0, 1)][1:]

    def body(x_ref, o_ref, ssem, rsem, lsem):
        x, y, c = _place()
        me = 4 * x + 2 * y + c
        local = pltpu.make_async_copy(x_ref, o_ref.at[me], lsem)
        local.start()
        peers = [(x ^ mx, y ^ my, c ^ mc) for (mx, my, mc) in masks]
        sends = []
        for j, peer in enumerate(peers):
            cp = pltpu.make_async_remote_copy(src_ref=x_ref, dst_ref=o_ref.at[me], send_sem=ssem.at[j], recv_sem=rsem.at[j],
                                              device_id=peer, device_id_type=MESH)
            cp.start()
            sends.append(cp)
        for j, (px, py, pc) in enumerate(peers):
            pltpu.make_async_remote_copy(src_ref=x_ref, dst_ref=o_ref.at[4 * px + 2 * py + pc], send_sem=ssem.at[j],
                                         recv_sem=rsem.at[j], device_id=(px, py, pc), device_id_type=MESH).wait_recv()
        for cp in sends:
            cp.wait_send()
        local.wait()

    return pl.pallas_call(
        body, name="gather_all", in_specs=[ANY], out_specs=ANY, out_shape=_sds((8, r, w), flat.dtype),
        scratch_shapes=[pltpu.SemaphoreType.DMA((7,)), pltpu.SemaphoreType.DMA((7,)), pltpu.SemaphoreType.DMA(())],
    )(flat)


def _sum8(parts):
    _, r, w = parts.shape

    def fn(i_refs, o_refs, _):
        acc = i_refs[0][0]
        for k in range(1, 8):
            acc = acc + i_refs[0][k]
        o_refs[0][...] = acc

    br = _tile(r, 512, 8)
    return _ew("sum8", fn, (r // br,), [parts], [pl.BlockSpec((8, br, w), lambda i: (0, i, 0))], [_sds((r, w))],
               [pl.BlockSpec((br, w), lambda i: (i, 0))])[0]


def _swap_halves(g):
    _, r, w = g.shape
    rh = r // 2

    def body(g_ref, o_ref, ssem, rsem):
        x, y, c = _place()
        cp = pltpu.make_async_remote_copy(src_ref=g_ref.at[:, pl.ds((1 - c) * rh, rh), :], dst_ref=o_ref, send_sem=ssem,
                                          recv_sem=rsem, device_id=(x, y, 1 - c), device_id_type=MESH)
        cp.start()
        cp.wait()

    return pl.pallas_call(
        body, name="swap_halves", in_specs=[ANY], out_specs=ANY, out_shape=_sds((4, rh, w), g.dtype),
        scratch_shapes=[pltpu.SemaphoreType.DMA(()), pltpu.SemaphoreType.DMA(())],
    )(g)


def _add_half(g, got, c_arr):
    _, r, w = g.shape
    rh = r // 2
    br = _tile(rh, 1280, 8)
    nb = rh // br

    def body(c_ref, g_ref, r_ref, o_ref):
        o_ref[...] = g_ref[...] + r_ref[...]

    return pl.pallas_call(
        body, name="add_half",
        grid_spec=pltpu.PrefetchScalarGridSpec(
            num_scalar_prefetch=1, grid=(4, nb),
            in_specs=[pl.BlockSpec((1, br, w), lambda s, i, c_ref: (s, c_ref[0] * nb + i, 0)),
                      pl.BlockSpec((1, br, w), lambda s, i, c_ref: (s, i, 0))],
            out_specs=pl.BlockSpec((1, br, w), lambda s, i, c_ref: (s, i, 0))),
        out_shape=_sds((4, rh, w)), compiler_params=_params(2),
    )(c_arr, g, got)


def _scatter_chips(p):
    _, rh, w = p.shape

    def body(p_ref, o_ref, ssem, rsem):
        x, y, c = _place()
        chips = [(1 - x, y), (x, 1 - y), (1 - x, 1 - y)]
        sends = []
        for j, (px, py) in enumerate(chips):
            cp = pltpu.make_async_remote_copy(src_ref=p_ref.at[2 * px + py], dst_ref=o_ref.at[j], send_sem=ssem.at[j],
                                              recv_sem=rsem.at[j], device_id=(px, py, c), device_id_type=MESH)
            cp.start()
            sends.append(cp)
        for cp in sends:
            cp.wait_recv()
        for cp in sends:
            cp.wait_send()

    return pl.pallas_call(
        body, name="scatter_chips", in_specs=[ANY], out_specs=ANY, out_shape=_sds((3, rh, w), p.dtype),
        scratch_shapes=[pltpu.SemaphoreType.DMA((3,)), pltpu.SemaphoreType.DMA((3,))],
    )(p)


def _add_chips(p, got, me_arr):
    _, rh, w = p.shape
    br = _tile(rh, 1280, 8)

    def body(me_ref, p_ref, r_ref, o_ref):
        o_ref[...] = ((p_ref[0] + r_ref[0]) + r_ref[1]) + r_ref[2]

    return pl.pallas_call(
        body, name="add_chips",
        grid_spec=pltpu.PrefetchScalarGridSpec(
            num_scalar_prefetch=1, grid=(rh // br,),
            in_specs=[pl.BlockSpec((1, br, w), lambda i, me_ref: (me_ref[0], i, 0)),
                      pl.BlockSpec((3, br, w), lambda i, me_ref: (0, i, 0))],
            out_specs=pl.BlockSpec((br, w), lambda i, me_ref: (i, 0))),
        out_shape=_sds((rh, w)), compiler_params=_params(1),
    )(me_arr, p, got)


def _join_halves(f):
    rh, w = f.shape

    def body(f_ref, o_ref, ssem, rsem, lsem):
        x, y, c = _place()
        local = pltpu.make_async_copy(f_ref, o_ref.at[pl.ds(c * rh, rh), :], lsem)
        local.start()
        cp = pltpu.make_async_remote_copy(src_ref=f_ref, dst_ref=o_ref.at[pl.ds(c * rh, rh), :], send_sem=ssem, recv_sem=rsem,
                                          device_id=(x, y, 1 - c), device_id_type=MESH)
        cp.start()
        pltpu.make_async_remote_copy(src_ref=f_ref, dst_ref=o_ref.at[pl.ds((1 - c) * rh, rh), :], send_sem=ssem, recv_sem=rsem,
                                     device_id=(x, y, 1 - c), device_id_type=MESH).wait_recv()
        cp.wait_send()
        local.wait()

    return pl.pallas_call(
        body, name="join_halves", in_specs=[ANY], out_specs=ANY, out_shape=_sds((2 * rh, w), f.dtype),
        scratch_shapes=[pltpu.SemaphoreType.DMA(()), pltpu.SemaphoreType.DMA(()), pltpu.SemaphoreType.DMA(())],
    )(f)


BIG = ("w_in", "w_conv_proj", "w_gdn_proj", "w_out", "w_ffn_in", "w_ffn_out")
COL_SHARDED = ("w_in", "w_ffn_in")
SMALL = ("b_in", "conv_dw_w", "conv_dw_b", "conv_ln_g", "conv_ln_b", "b_conv_proj", "short_conv_w", "a_log", "dt_bias",
         "gdn_norm_w", "ln1_g", "ln1_b", "ln2_g", "ln2_b")
SMALL_SHARDED = ("conv_dw_w", "short_conv_w")
ORDER = ("w_in", "b_in", "conv_dw_w", "conv_dw_b", "conv_ln_g", "conv_ln_b", "w_conv_proj", "b_conv_proj", "short_conv_w",
         "a_log", "dt_bias", "gdn_norm_w", "w_gdn_proj", "w_out", "ln1_g", "ln1_b", "w_ffn_in", "w_ffn_out", "ln2_g", "ln2_b")


def _flat_rows(shapes, fw):
    rows = [int(np.prod(sh)) // fw for sh in shapes]
    total = sum(rows)
    pad = (-total) % 32
    return rows, total + pad, pad


def _row(v):
    return v.reshape(1, -1)


def _layer_fwd(x, wl, alpha):
    d = x.shape[1]
    z = _mm("mm_z", x, wl["w_main"], "nn", bias=wl["b_main"])
    ba = _mm("mm_ba", x, wl["w_ba"], "nn", bias=wl["b_ba"])
    c3 = _conv_fwd(z, wl["conv_dw_w"], wl["conv_dw_b"], wl["conv_ln_g"], wl["conv_ln_b"], d)
    yc = _mm("mm_proj", c3, wl["w_conv_proj"], "nn", bias=wl["b_conv_proj"])
    qkv = _sconv_fwd(z, wl["short_conv_w"], d, 2)
    s = x.shape[0]
    nh = d // HEAD
    bat = jnp.transpose(ba[:, :2 * nh].reshape(s // CHUNK, CHUNK, 2 * nh), (0, 2, 1))
    o, states = _gdn_fwd(qkv, ba, bat, wl["hp"])
    og = _gdn_post_fwd(o, z, wl["gdn_norm_w"], d, 5)
    yg = _mm("mm_proj", og, wl["w_gdn_proj"], "nn", bias=wl["zero_bias"])
    m = _merge_fwd(z, yc, yg, d, 6, 7)
    mix = _mm("mm_proj", m, wl["w_out"], "nn", bias=wl["zero_bias"])
    x1 = _ln_res_fwd(x, mix, wl["ln1_g"], wl["ln1_b"], alpha)
    hg = _mm("mm_ffn_in", x1, wl["w_ffn_g"], "nn")
    hu = _mm("mm_ffn_in", x1, wl["w_ffn_u"], "nn")
    f = _swiglu_fwd(hg, hu)
    ffn = _mm("mm_ffn_out", f, wl["w_ffn_out"], "nn")
    x2 = _ln_res_fwd(x1, ffn, wl["ln2_g"], wl["ln2_b"], alpha)
    saved = dict(x=x, z=z, ba=ba, bat=bat, c3=c3, yc=yc, qkv=qkv, o=o, states=states, og=og, yg=yg, m=m, mix=mix, x1=x1,
                 hg=hg, hu=hu, f=f, ffn=ffn)
    return x2, saved


def _layer_bwd(dy, sv, wl, alpha):
    x, z, x1 = sv["x"], sv["z"], sv["x1"]
    s, d = x.shape
    nh = d // HEAD
    g = {}
    dr2, g["ln2_g"], g["ln2_b"] = _ln_res_bwd(x1, sv["ffn"], wl["ln2_g"], dy, alpha)
    df = _mm("mm_dffn_out", dr2, wl["w_ffn_out"], "nt")
    g["w_ffn_out"] = _mm("mm_gw_ffn_out", sv["f"], dr2, "tn")
    dhg, dhu = _swiglu_bwd(sv["hg"], sv["hu"], df)
    t = _mm("mm_dffn_in", dhg, wl["w_ffn_g"], "nt", res=dr2, res_scale=alpha)
    dx1 = _mm("mm_dffn_in", dhu, wl["w_ffn_u"], "nt", res=t)
    g["w_ffn_in"] = jnp.concatenate([_mm("mm_gw_ffn_in", x1, dhg, "tn"), _mm("mm_gw_ffn_in", x1, dhu, "tn")], axis=1)
    dr1, g["ln1_g"], g["ln1_b"] = _ln_res_bwd(x, sv["mix"], wl["ln1_g"], dx1, alpha)
    dm = _mm("mm_dproj", dr1, wl["w_out"], "nt")
    g["w_out"] = _mm("mm_gw_proj", sv["m"], dr1, "tn")
    dyc, dyg, dga, dgb, g["b_conv_proj"] = _merge_bwd(dm, z, sv["yc"], sv["yg"], d, 6, 7)
    dc3 = _mm("mm_dproj", dyc, wl["w_conv_proj"], "nt")
    g["w_conv_proj"] = _mm("mm_gw_proj", sv["c3"], dyc, "tn")
    dog = _mm("mm_dproj", dyg, wl["w_gdn_proj"], "nt")
    g["w_gdn_proj"] = _mm("mm_gw_proj", sv["og"], dyg, "tn")
    dglu_a, dglu_b, g["conv_dw_w"], g["conv_dw_b"], g["conv_ln_g"], g["conv_ln_b"] = _conv_bwd(
        z, dc3, wl["conv_dw_w"], wl["conv_dw_b"], wl["conv_ln_g"], wl["conv_ln_b"], d)
    do, dzg, g["gdn_norm_w"] = _gdn_post_bwd(dog, sv["o"], z, wl["gdn_norm_w"], d, 5)
    dqkv, dbat, dhp = _gdn_bwd(sv["qkv"], sv["ba"], sv["bat"], wl["hp"], sv["states"], do)
    g["a_log"] = dhp[:, 0, 0]
    g["dt_bias"] = dhp[:, 1, 0]
    dzq, g["short_conv_w"] = _sconv_bwd(z, dqkv, wl["short_conv_w"], d, 2)
    dz = jnp.concatenate([dglu_a, dglu_b, dzq[0], dzq[1], dzq[2], dzg, dga, dgb], axis=1)
    dba = jnp.transpose(dbat[:, :, :2, :], (1, 3, 2, 0)).reshape(s, 2 * nh)
    dba = jnp.pad(dba, ((0, 0), (0, LANE - 2 * nh)))
    t = _mm("mm_dba", dba, wl["w_ba"], "nt", res=dr1, res_scale=alpha)
    dx = _mm("mm_dz", dz, wl["w_main"], "nt", res=t)
    gw_main = _mm("mm_gw_main", x, dz, "tn")
    gw_ba = _mm("mm_gw_ba", x, dba, "tn")
    g["w_in"] = jnp.concatenate([gw_main[:, :6 * d], gw_ba[:, :2 * nh], gw_main[:, 6 * d:]], axis=1)
    dbz = _col_sums("colsum_dz", dz)
    dbb = _col_sums("colsum_dba", dba)
    g["b_in"] = jnp.concatenate([dbz[:, :6 * d], dbb[:, :2 * nh], dbz[:, 6 * d:]], axis=1)
    return dx, g


def kernel(x, w_in, b_in, conv_dw_w, conv_dw_b, conv_ln_g, conv_ln_b, w_conv_proj, b_conv_proj, short_conv_w, a_log, dt_bias, gdn_norm_w, w_gdn_proj, w_out, ln1_g, ln1_b, w_ffn_in, w_ffn_out, ln2_g, ln2_b, loss_target, m_w_in, m_b_in, m_conv_dw_w, m_conv_dw_b, m_conv_ln_g, m_conv_ln_b, m_w_conv_proj, m_b_conv_proj, m_short_conv_w, m_a_log, m_dt_bias, m_gdn_norm_w, m_w_gdn_proj, m_w_out, m_ln1_g, m_ln1_b, m_w_ffn_in, m_w_ffn_out, m_ln2_g, m_ln2_b, v_w_in, v_b_in, v_conv_dw_w, v_conv_dw_b, v_conv_ln_g, v_conv_ln_b, v_w_conv_proj, v_b_conv_proj, v_short_conv_w, v_a_log, v_dt_bias, v_gdn_norm_w, v_w_gdn_proj, v_w_out, v_ln1_g, v_ln1_b, v_w_ffn_in, v_w_ffn_out, v_ln2_g, v_ln2_b):
    wts = dict(w_in=w_in, b_in=b_in, conv_dw_w=conv_dw_w, conv_dw_b=conv_dw_b, conv_ln_g=conv_ln_g, conv_ln_b=conv_ln_b,
               w_conv_proj=w_conv_proj, b_conv_proj=b_conv_proj, short_conv_w=short_conv_w, a_log=a_log, dt_bias=dt_bias,
               gdn_norm_w=gdn_norm_w, w_gdn_proj=w_gdn_proj, w_out=w_out, ln1_g=ln1_g, ln1_b=ln1_b, w_ffn_in=w_ffn_in,
               w_ffn_out=w_ffn_out, ln2_g=ln2_g, ln2_b=ln2_b)
    mom = dict(w_in=m_w_in, b_in=m_b_in, conv_dw_w=m_conv_dw_w, conv_dw_b=m_conv_dw_b, conv_ln_g=m_conv_ln_g,
               conv_ln_b=m_conv_ln_b, w_conv_proj=m_w_conv_proj, b_conv_proj=m_b_conv_proj, short_conv_w=m_short_conv_w,
               a_log=m_a_log, dt_bias=m_dt_bias, gdn_norm_w=m_gdn_norm_w, w_gdn_proj=m_w_gdn_proj, w_out=m_w_out,
               ln1_g=m_ln1_g, ln1_b=m_ln1_b, w_ffn_in=m_w_ffn_in, w_ffn_out=m_w_ffn_out, ln2_g=m_ln2_g, ln2_b=m_ln2_b)
    var = dict(w_in=v_w_in, b_in=v_b_in, conv_dw_w=v_conv_dw_w, conv_dw_b=v_conv_dw_b, conv_ln_g=v_conv_ln_g,
               conv_ln_b=v_conv_ln_b, w_conv_proj=v_w_conv_proj, b_conv_proj=v_b_conv_proj, short_conv_w=v_short_conv_w,
               a_log=v_a_log, dt_bias=v_dt_bias, gdn_norm_w=v_gdn_norm_w, w_gdn_proj=v_w_gdn_proj, w_out=v_w_out,
               ln1_g=v_ln1_g, ln1_b=v_ln1_b, w_ffn_in=v_w_ffn_in, w_ffn_out=v_w_ffn_out, ln2_g=v_ln2_g, ln2_b=v_ln2_b)

    depth = w_in.shape[0]
    _, s, d = x.shape
    nh = d // HEAD
    alpha = float((2.0 * depth) ** 0.25)
    xi, yi, ci = _place()
    chip = 2 * xi + yi
    c_arr = jnp.reshape(ci, (1,)).astype(jnp.int32)
    chip_arr = jnp.reshape(chip, (1,)).astype(jnp.int32)

    shard_shapes = [wts[n].shape[1:] for n in BIG]
    fw = FLAT_W if all(int(np.prod(sh)) % FLAT_W == 0 for sh in shard_shapes) else LANE
    rows, total_rows, pad_rows = _flat_rows(shard_shapes, fw)
    layers = []
    for l in range(depth):
        parts = [wts[n][l].reshape(-1, fw) for n in BIG]
        if pad_rows:
            parts.append(jnp.zeros((pad_rows, fw), F32))
        gathered = _gather_chips(jnp.concatenate(parts, axis=0).astype(BF16))
        full, off = {}, 0
        for n, sh, nr in zip(BIG, shard_shapes, rows):
            blk = gathered[:, off:off + nr, :].reshape((4,) + tuple(sh))
            off += nr
            if n in COL_SHARDED:
                full[n] = jnp.transpose(blk, (1, 0, 2)).reshape(sh[0], 4 * sh[1])
            else:
                full[n] = blk.reshape(4 * sh[0], sh[1])
        wi = full["w_in"]
        bi = b_in[l]
        f_ff = full["w_ffn_in"].shape[1] // 2
        layers.append(dict(
            w_main=jnp.concatenate([wi[:, :6 * d], wi[:, 6 * d + 2 * nh:]], axis=1),
            w_ba=jnp.pad(wi[:, 6 * d:6 * d + 2 * nh], ((0, 0), (0, LANE - 2 * nh))),
            b_main=_row(jnp.concatenate([bi[:6 * d], bi[6 * d + 2 * nh:]])),
            b_ba=_row(jnp.pad(bi[6 * d:6 * d + 2 * nh], (0, LANE - 2 * nh))),
            w_conv_proj=full["w_conv_proj"], w_gdn_proj=full["w_gdn_proj"], w_out=full["w_out"],
            w_ffn_g=full["w_ffn_in"][:, :f_ff], w_ffn_u=full["w_ffn_in"][:, f_ff:], w_ffn_out=full["w_ffn_out"],
            b_conv_proj=_row(b_conv_proj[l]), zero_bias=jnp.zeros((1, d), F32),
            conv_dw_b=_row(conv_dw_b[l]), conv_ln_g=_row(conv_ln_g[l]), conv_ln_b=_row(conv_ln_b[l]),
            gdn_norm_w=_row(gdn_norm_w[l]), ln1_g=_row(ln1_g[l]), ln1_b=_row(ln1_b[l]), ln2_g=_row(ln2_g[l]), ln2_b=_row(ln2_b[l]),
            hp=jnp.concatenate([jnp.broadcast_to(a_log[l][:, None, None], (nh, 1, LANE)),
                                jnp.broadcast_to(dt_bias[l][:, None, None], (nh, 1, LANE)),
                                jnp.zeros((nh, 6, LANE), F32)], axis=1),
        ))

    kw, cs = conv_dw_w.shape[1], conv_dw_w.shape[2]
    ks, ss = short_conv_w.shape[1], short_conv_w.shape[2]
    small_w = jnp.concatenate([conv_dw_w.reshape(depth * kw, cs), jnp.zeros(((-depth * kw) % 8, cs), F32)], axis=0)
    cw_all = _gather_chips(small_w)[:, :depth * kw].reshape(4, depth, kw, cs)
    cw_all = jnp.transpose(cw_all, (1, 2, 0, 3)).reshape(depth, kw, 4 * cs)
    small_s = jnp.concatenate([short_conv_w.reshape(depth * ks, ss), jnp.zeros(((-depth * ks) % 8, ss), F32)], axis=0)
    sw_all = _gather_chips(small_s)[:, :depth * ks].reshape(4, depth, ks, ss)
    sw_all = jnp.transpose(sw_all, (1, 2, 0, 3)).reshape(depth, ks, 4 * ss)
    for l in range(depth):
        layers[l]["conv_dw_w"] = cw_all[l]
        layers[l]["short_conv_w"] = sw_all[l]

    h = x[0]
    saved = []
    for l in range(depth):
        h, sv = _layer_fwd(h, layers[l], alpha)
        saved.append(sv)
    loss_blk, dy = _loss_fwd_bwd(h, loss_target[0])
    loss = lax.psum(0.5 * loss_blk[0, 0], ("x", "y", "c"))

    grads = [None] * depth
    shard_grads = [None] * depth
    for l in reversed(range(depth)):
        dy, g = _layer_bwd(dy, saved[l], layers[l], alpha)
        grads[l] = g
        parts = []
        for n, sh in zip(BIG, shard_shapes):
            gf = g[n]
            if n in COL_SHARDED:
                gs = jnp.transpose(gf.reshape(sh[0], 4, sh[1]), (1, 0, 2))
            else:
                gs = gf.reshape(4, sh[0], sh[1])
            parts.append(gs.reshape(4, -1, fw))
        if pad_rows:
            parts.append(jnp.zeros((4, pad_rows, fw), F32))
        gflat = jnp.concatenate(parts, axis=1)
        pair = _add_half(gflat, _swap_halves(gflat), c_arr)
        mine = _add_chips(pair, _scatter_chips(pair), chip_arr)
        shard = _join_halves(mine)
        sg, off = {}, 0
        for n, sh, nr in zip(BIG, shard_shapes, rows):
            sg[n] = shard[off:off + nr].reshape(sh)
            off += nr
        shard_grads[l] = sg
    grad_x = dy[None]

    small_parts = [jnp.concatenate([grads[l][n].reshape(-1) for l in range(depth)]) for n in SMALL]
    sizes = [int(p.shape[0]) for p in small_parts]
    flat = jnp.concatenate(small_parts)
    nflat = int(flat.shape[0])
    flat = jnp.pad(flat, (0, (-nflat) % (8 * LANE))).reshape(-1, LANE)
    tot = _sum8(_gather_all(flat)).reshape(-1)
    small_g, off = {}, 0
    for n, sz in zip(SMALL, sizes):
        full = tot[off:off + sz]
        off += sz
        if n == "conv_dw_w":
            full = lax.dynamic_slice_in_dim(full.reshape(depth, kw, 4 * cs), chip * cs, cs, axis=2)
        elif n == "short_conv_w":
            full = lax.dynamic_slice_in_dim(full.reshape(depth, ks, 4 * ss), chip * ss, ss, axis=2)
        small_g[n] = full.reshape(wts[n].shape)

    out_g, out_d, out_m, out_v = {}, {}, {}, {}
    for n in BIG:
        gfull = jnp.stack([shard_grads[l][n] for l in range(depth)])
        sh = wts[n].shape
        two = (sh[0] * sh[1], sh[2])
        dl, nm, nv = _adamw("adamw_" + n, wts[n].reshape(two), gfull.reshape(two), mom[n].reshape(two), var[n].reshape(two))
        out_g[n], out_d[n], out_m[n], out_v[n] = gfull, dl.reshape(sh), nm.reshape(sh), nv.reshape(sh)

    def pack(src):
        v = jnp.concatenate([src[n].reshape(-1) for n in SMALL])
        return jnp.pad(v, (0, (-int(v.shape[0])) % (8 * LANE))).reshape(-1, LANE)

    dl, nm, nv = _adamw("adamw_small", pack(wts), pack(small_g), pack(mom), pack(var))
    off = 0
    for n in SMALL:
        sz = int(np.prod(wts[n].shape))
        out_g[n] = small_g[n]
        out_d[n] = dl.reshape(-1)[off:off + sz].reshape(wts[n].shape)
        out_m[n] = nm.reshape(-1)[off:off + sz].reshape(wts[n].shape)
        out_v[n] = nv.reshape(-1)[off:off + sz].reshape(wts[n].shape)
        off += sz

    return (loss, grad_x, *[out_g[n] for n in ORDER], *[out_d[n] for n in ORDER], *[out_m[n] for n in ORDER],
            *[out_v[n] for n in ORDER])
```

```python
import functools

import jax
import jax.numpy as jnp
import numpy as np
from jax import lax
from jax.experimental import pallas as pl
from jax.experimental.pallas import tpu as pltpu

F32 = jnp.float32
BF16 = jnp.bfloat16
MESH = pl.DeviceIdType.MESH

LN_EPS = 1e-5
L2_EPS = 1e-6
CHUNK = 64
HEAD = 128
CONV_HALO = 32
SHORT_HALO = 8
ROW_BLOCK = 256
VMEM_LIMIT = 56 * 1024 * 1024
LANE = 128
FLAT_W = 1024

ADAM_LR, ADAM_B1, ADAM_B2, ADAM_EPS, ADAM_WD, ADAM_STEP = 0.001, 0.9, 0.999, 1e-08, 0.01, 10

NN = ((1,), (0,))
NT = ((1,), (1,))
TN = ((0,), (0,))


def _params(n_axes):
    return pltpu.CompilerParams(dimension_semantics=("arbitrary",) * n_axes, vmem_limit_bytes=VMEM_LIMIT)


def _tile(dim, pref, unit=LANE):
    if dim <= pref:
        return dim
    best = None
    for t in range(unit, pref + 1, unit):
        if dim % t == 0:
            best = t
    assert best is not None, (dim, pref, unit)
    return best


def _dotb(a, b, dims):
    return lax.dot_general(a.astype(BF16), b.astype(BF16), (dims, ((), ())), preferred_element_type=F32)


def _split(a):
    hi = a.astype(BF16)
    return hi, (a - hi.astype(F32)).astype(BF16)


def _sig(x):
    return jax.nn.sigmoid(x)


def _dsilu(x):
    s = _sig(x)
    return s * (1.0 + x * (1.0 - s))


def _softplus(x):
    return jnp.maximum(x, 0.0) + jnp.log(1.0 + jnp.exp(-jnp.abs(x)))


def _iota(shape, dim):
    return lax.broadcasted_iota(jnp.int32, shape, dim)


def _accum(ref, val, first):
    @pl.when(first)
    def _():
        ref[...] = val

    @pl.when(jnp.logical_not(first))
    def _():
        ref[...] += val


def _mm(name, a, b, mode, out_dtype=F32, bias=None, res=None, res_scale=1.0, tm=1024, tn=1024, tk=2048):
    if mode == "nn":
        (m, k), (k2, n) = a.shape, b.shape
    elif mode == "tn":
        (k, m), (k2, n) = a.shape, b.shape
    else:
        (m, k), (n, k2) = a.shape, b.shape
    assert k == k2, (name, a.shape, b.shape)
    tm, tn, tk = _tile(m, tm), _tile(n, tn), _tile(k, tk)
    nk = k // tk
    dims = {"nn": NN, "tn": TN, "nt": NT}[mode]
    a_spec = {"nn": pl.BlockSpec((tm, tk), lambda i, j, kk: (i, kk)),
              "tn": pl.BlockSpec((tk, tm), lambda i, j, kk: (kk, i)),
              "nt": pl.BlockSpec((tm, tk), lambda i, j, kk: (i, kk))}[mode]
    b_spec = {"nn": pl.BlockSpec((tk, tn), lambda i, j, kk: (kk, j)),
              "tn": pl.BlockSpec((tk, tn), lambda i, j, kk: (kk, j)),
              "nt": pl.BlockSpec((tn, tk), lambda i, j, kk: (j, kk))}[mode]
    ins, in_specs = [a, b], [a_spec, b_spec]
    if bias is not None:
        ins.append(bias)
        in_specs.append(pl.BlockSpec((1, tn), lambda i, j, kk: (0, j)))
    if res is not None:
        ins.append(res)
        in_specs.append(pl.BlockSpec((tm, tn), lambda i, j, kk: (i, j)))
    has_bias, has_res = bias is not None, res is not None

    def body(*refs):
        a_ref, b_ref = refs[0], refs[1]
        pos = 2
        bias_ref = res_ref = None
        if has_bias:
            bias_ref = refs[pos]
            pos += 1
        if has_res:
            res_ref = refs[pos]
            pos += 1
        o_ref = refs[pos]
        part = _dotb(a_ref[...], b_ref[...], dims)

        def finish(r):
            if has_bias:
                r = r + bias_ref[...]
            if has_res:
                r = r + res_scale * res_ref[...]
            o_ref[...] = r.astype(out_dtype)

        if nk == 1:
            finish(part)
        else:
            acc_ref = refs[pos + 1]
            kk = pl.program_id(2)
            _accum(acc_ref, part, kk == 0)

            @pl.when(kk == nk - 1)
            def _():
                finish(acc_ref[...])

    return pl.pallas_call(
        body, name=name, grid=(m // tm, n // tn, nk), in_specs=in_specs,
        out_specs=pl.BlockSpec((tm, tn), lambda i, j, kk: (i, j)),
        out_shape=jax.ShapeDtypeStruct((m, n), out_dtype),
        scratch_shapes=[pltpu.VMEM((tm, tn), F32)] if nk > 1 else [], compiler_params=_params(3),
    )(*ins)


def _ew(name, fn, grid, ins, in_specs, out_shapes, out_specs, scratch=()):
    n_in, n_out = len(ins), len(out_shapes)

    def body(*refs):
        fn(refs[:n_in], refs[n_in:n_in + n_out], refs[n_in + n_out:])

    out = pl.pallas_call(
        body, name=name, grid=grid, in_specs=list(in_specs), out_specs=list(out_specs),
        out_shape=list(out_shapes), scratch_shapes=list(scratch), compiler_params=_params(len(grid)),
    )(*ins)
    return out


def _sds(shape, dtype=F32):
    return jax.ShapeDtypeStruct(tuple(shape), dtype)


def _rows(bs, w, col=0):
    return pl.BlockSpec((bs, w), lambda i, col=col: (i, col))


def _par(r, w):
    return pl.BlockSpec((r, w), lambda i: (0, 0))


def _ln_stats(r):
    mu = jnp.mean(r, axis=-1, keepdims=True)
    xc = r - mu
    var = jnp.mean(xc * xc, axis=-1, keepdims=True)
    rstd = lax.rsqrt(var + LN_EPS)
    return xc * rstd, rstd


def _ln_back(xh, rstd, g, dy):
    dxh = dy * g
    m1 = jnp.mean(dxh, axis=-1, keepdims=True)
    m2 = jnp.mean(dxh * xh, axis=-1, keepdims=True)
    return rstd * (dxh - m1 - xh * m2)


def _colsum(v):
    return jnp.sum(v, axis=0, keepdims=True)


def _ln_res_fwd(x, sub, g, b, alpha):
    s, d = x.shape
    bs = min(ROW_BLOCK, s)

    def fn(i_refs, o_refs, _):
        x_ref, s_ref, g_ref, b_ref = i_refs
        xh, _r = _ln_stats(alpha * x_ref[...] + s_ref[...])
        y = xh * g_ref[...] + b_ref[...]
        o_refs[0][...] = y
        o_refs[1][...] = y.astype(BF16)

    return _ew("ln_res_fwd", fn, (s // bs,), [x, sub, g, b], [_rows(bs, d), _rows(bs, d), _par(1, d), _par(1, d)],
               [_sds((s, d)), _sds((s, d), BF16)], [_rows(bs, d), _rows(bs, d)])


def _ln_res_bwd(x, sub, g, dy, alpha):
    s, d = x.shape
    bs = min(ROW_BLOCK, s)

    def fn(i_refs, o_refs, _):
        x_ref, s_ref, g_ref, dy_ref = i_refs
        dr_ref, drb_ref, dg_ref, db_ref = o_refs
        first = pl.program_id(0) == 0
        xh, rstd = _ln_stats(alpha * x_ref[...] + s_ref[...])
        dy_v = dy_ref[...]
        dr = _ln_back(xh, rstd, g_ref[...], dy_v)
        dr_ref[...] = dr
        drb_ref[...] = dr.astype(BF16)
        _accum(dg_ref, _colsum(dy_v * xh), first)
        _accum(db_ref, _colsum(dy_v), first)

    return _ew("ln_res_bwd", fn, (s // bs,), [x, sub, g, dy], [_rows(bs, d), _rows(bs, d), _par(1, d), _rows(bs, d)],
               [_sds((s, d)), _sds((s, d), BF16), _sds((1, d)), _sds((1, d))],
               [_rows(bs, d), _rows(bs, d), _par(1, d), _par(1, d)])


def _loss_fwd_bwd(y, t):
    s, d = y.shape
    bs = min(ROW_BLOCK, s)

    def fn(i_refs, o_refs, _):
        err = i_refs[0][...] - i_refs[1][...]
        o_refs[1][...] = err * (1.0 / d)
        tot = jnp.sum(jnp.sum(err * err, axis=1, keepdims=True), axis=0, keepdims=True) * (1.0 / d)
        _accum(o_refs[0], jnp.broadcast_to(tot, (8, LANE)), pl.program_id(0) == 0)

    return _ew("loss", fn, (s // bs,), [y, t], [_rows(bs, d), _rows(bs, d)],
               [_sds((8, LANE)), _sds((s, d))], [_par(8, LANE), _rows(bs, d)])


def _conv_fwd(z, cw, cb, g, b, d):
    s = z.shape[0]
    kw = cw.shape[0]
    bs = min(ROW_BLOCK, s)
    hb = CONV_HALO
    r = bs // hb

    def prev(col):
        return pl.BlockSpec((hb, d), lambda i: (jnp.maximum(i * r - 1, 0), col))

    def fn(i_refs, o_refs, scr):
        a_ref, b_ref, ap_ref, bp_ref, w_ref, cb_ref, g_ref, be_ref = i_refs
        ext = scr[0]
        i = pl.program_id(0)
        ext[pl.ds(0, hb), :] = jnp.where(i > 0, ap_ref[...] * _sig(bp_ref[...]), 0.0)
        ext[pl.ds(hb, bs), :] = a_ref[...] * _sig(b_ref[...])
        acc = jnp.zeros((bs, d), F32) + cb_ref[...]
        for j in range(kw):
            acc = acc + w_ref[pl.ds(j, 1), :] * ext[pl.ds(hb - (kw - 1) + j, bs), :]
        xh, _r = _ln_stats(acc)
        c2 = xh * g_ref[...] + be_ref[...]
        o_refs[0][...] = (c2 * _sig(c2)).astype(BF16)

    return _ew("conv_fwd", fn, (s // bs,), [z, z, z, z, cw, cb, g, b],
               [_rows(bs, d, 0), _rows(bs, d, 1), prev(0), prev(1), _par(kw, d), _par(1, d), _par(1, d), _par(1, d)],
               [_sds((s, d), BF16)], [_rows(bs, d)], scratch=[pltpu.VMEM((hb + bs, d), F32)])[0]


def _conv_bwd(z, dc3, cw, cb, g, b, d):
    s = z.shape[0]
    kw = cw.shape[0]
    bs = min(ROW_BLOCK, s)
    hb = CONV_HALO
    r = bs // hb
    nrow = s // bs
    last_h = s // hb - 1

    def prev(col):
        return pl.BlockSpec((hb, d), lambda i: (jnp.maximum(i * r - 1, 0), col))

    def nxt(col):
        return pl.BlockSpec((hb, d), lambda i: (jnp.minimum((i + 1) * r, last_h), col))

    def fn(i_refs, o_refs, scr):
        a_ref, b_ref, ap_ref, bp_ref, an_ref, bn_ref, d_ref, dn_ref, w_ref, cb_ref, g_ref, be_ref = i_refs
        da_ref, db_ref, dw_ref, dcb_ref, dg_ref, dbe_ref = o_refs
        ext, extd = scr
        i = pl.program_id(0)
        first = i == 0
        more = i < nrow - 1
        a_v, b_v = a_ref[...], b_ref[...]
        sb = _sig(b_v)
        ext[pl.ds(0, hb), :] = jnp.where(i > 0, ap_ref[...] * _sig(bp_ref[...]), 0.0)
        ext[pl.ds(hb, bs), :] = a_v * sb
        ext[pl.ds(hb + bs, hb), :] = jnp.where(more, an_ref[...] * _sig(bn_ref[...]), 0.0)
        n1 = bs + hb
        acc = jnp.zeros((n1, d), F32) + cb_ref[...]
        for j in range(kw):
            acc = acc + w_ref[pl.ds(j, 1), :] * ext[pl.ds(hb - (kw - 1) + j, n1), :]
        xh, rstd = _ln_stats(acc)
        c2 = xh * g_ref[...] + be_ref[...]
        dc3_e = jnp.concatenate([d_ref[...], jnp.where(more, dn_ref[...], 0.0)], axis=0)
        dc2 = dc3_e * _dsilu(c2)
        dc1 = _ln_back(xh, rstd, g_ref[...], dc2)
        extd[...] = dc1
        _accum(dg_ref, _colsum((dc2 * xh)[:bs]), first)
        _accum(dbe_ref, _colsum(dc2[:bs]), first)
        _accum(dcb_ref, _colsum(dc1[:bs]), first)

        @pl.when(first)
        def _():
            dw_ref[...] = jnp.zeros_like(dw_ref)

        dc1_own = extd[pl.ds(0, bs), :]
        dc0 = jnp.zeros((bs, d), F32)
        for j in range(kw):
            wj = w_ref[pl.ds(j, 1), :]
            dc0 = dc0 + wj * extd[pl.ds(kw - 1 - j, bs), :]
            dw_ref[pl.ds(j, 1), :] += _colsum(dc1_own * ext[pl.ds(hb - (kw - 1) + j, bs), :])
        da_ref[...] = (dc0 * sb).astype(BF16)
        db_ref[...] = (dc0 * a_v * sb * (1.0 - sb)).astype(BF16)

    return _ew("conv_bwd", fn, (nrow,), [z, z, z, z, z, z, dc3, dc3, cw, cb, g, b],
               [_rows(bs, d, 0), _rows(bs, d, 1), prev(0), prev(1), nxt(0), nxt(1), _rows(bs, d), nxt(0),
                _par(kw, d), _par(1, d), _par(1, d), _par(1, d)],
               [_sds((s, d), BF16), _sds((s, d), BF16), _sds((kw, d)), _sds((1, d)), _sds((1, d)), _sds((1, d))],
               [_rows(bs, d), _rows(bs, d), _par(kw, d), _par(1, d), _par(1, d), _par(1, d)],
               scratch=[pltpu.VMEM((bs + 2 * hb, d), F32), pltpu.VMEM((bs + hb, d), F32)])


def _sconv_fwd(z, sw, d, col0):
    s = z.shape[0]
    kw = sw.shape[0]
    bs = min(ROW_BLOCK, s)
    hb = SHORT_HALO
    r = bs // hb

    def fn(i_refs, o_refs, scr):
        x_ref, xp_ref, w_ref = i_refs
        ext = scr[0]
        i = pl.program_id(1)
        ext[pl.ds(0, hb), :] = jnp.where(i > 0, xp_ref[...], 0.0)
        ext[pl.ds(hb, bs), :] = x_ref[...]
        acc = jnp.zeros((bs, d), F32)
        for j in range(kw):
            acc = acc + w_ref[pl.ds(j, 1), :] * ext[pl.ds(hb - (kw - 1) + j, bs), :]
        o_refs[0][0] = acc * _sig(acc)

    return _ew("sconv_fwd", fn, (3, s // bs), [z, z, sw],
               [pl.BlockSpec((bs, d), lambda sg, i: (i, col0 + sg)),
                pl.BlockSpec((hb, d), lambda sg, i: (jnp.maximum(i * r - 1, 0), col0 + sg)),
                pl.BlockSpec((kw, d), lambda sg, i: (0, sg))],
               [_sds((3, s, d))], [pl.BlockSpec((1, bs, d), lambda sg, i: (sg, i, 0))],
               scratch=[pltpu.VMEM((hb + bs, d), F32)])[0]


def _sconv_bwd(z, dy, sw, d, col0):
    s = z.shape[0]
    kw = sw.shape[0]
    bs = min(ROW_BLOCK, s)
    hb = SHORT_HALO
    r = bs // hb
    nrow = s // bs
    last_h = s // hb - 1

    def fn(i_refs, o_refs, scr):
        x_ref, xp_ref, xn_ref, d_ref, dn_ref, w_ref = i_refs
        dx_ref, dw_ref = o_refs
        ext, extd = scr
        i = pl.program_id(1)
        more = i < nrow - 1
        ext[pl.ds(0, hb), :] = jnp.where(i > 0, xp_ref[...], 0.0)
        ext[pl.ds(hb, bs), :] = x_ref[...]
        ext[pl.ds(hb + bs, hb), :] = jnp.where(more, xn_ref[...], 0.0)
        n1 = bs + hb
        pre = jnp.zeros((n1, d), F32)
        for j in range(kw):
            pre = pre + w_ref[pl.ds(j, 1), :] * ext[pl.ds(hb - (kw - 1) + j, n1), :]
        dy_e = jnp.concatenate([d_ref[0], jnp.where(more, dn_ref[0], 0.0)], axis=0)
        extd[...] = dy_e * _dsilu(pre)

        @pl.when(i == 0)
        def _():
            dw_ref[...] = jnp.zeros_like(dw_ref)

        dp_own = extd[pl.ds(0, bs), :]
        dx = jnp.zeros((bs, d), F32)
        for j in range(kw):
            dx = dx + w_ref[pl.ds(j, 1), :] * extd[pl.ds(kw - 1 - j, bs), :]
            dw_ref[pl.ds(j, 1), :] += _colsum(dp_own * ext[pl.ds(hb - (kw - 1) + j, bs), :])
        dx_ref[0] = dx.astype(BF16)

    return _ew("sconv_bwd", fn, (3, nrow), [z, z, z, dy, dy, sw],
               [pl.BlockSpec((bs, d), lambda sg, i: (i, col0 + sg)),
                pl.BlockSpec((hb, d), lambda sg, i: (jnp.maximum(i * r - 1, 0), col0 + sg)),
                pl.BlockSpec((hb, d), lambda sg, i: (jnp.minimum((i + 1) * r, last_h), col0 + sg)),
                pl.BlockSpec((1, bs, d), lambda sg, i: (sg, i, 0)),
                pl.BlockSpec((1, hb, d), lambda sg, i: (sg, jnp.minimum((i + 1) * r, last_h), 0)),
                pl.BlockSpec((kw, d), lambda sg, i: (0, sg))],
               [_sds((3, s, d), BF16), _sds((kw, 3 * d))],
               [pl.BlockSpec((1, bs, d), lambda sg, i: (sg, i, 0)), pl.BlockSpec((kw, d), lambda sg, i: (0, sg))],
               scratch=[pltpu.VMEM((bs + 2 * hb, d), F32), pltpu.VMEM((bs + hb, d), F32)])


GDN_HEADS = 16

BNN = (((2,), (1,)), ((0,), (0,)))
BNT = (((2,), (2,)), ((0,), (0,)))
BTN = (((1,), (1,)), ((0,), (0,)))


def _bdot(a, b, dn):
    return lax.dot_general(a.astype(BF16), b.astype(BF16), dn, preferred_element_type=F32)


def _bdotf(a, b, dn):
    ah, al = _split(a)
    bh, bl = _split(b)

    def dot(u, v):
        return lax.dot_general(u, v, dn, preferred_element_type=F32)

    return dot(ah, bh) + (dot(ah, bl) + dot(al, bh))


def _gdn_pre(qkv_ref, ba_ref, bat_ref, hp_ref, hb):
    g0 = pl.program_id(0) * hb
    c = CHUNK

    def heads(part):
        return jnp.stack([qkv_ref[part, :, pl.ds(hh * HEAD, HEAD)] for hh in range(hb)], axis=0)

    qr, kr, v = heads(0), heads(1), heads(2)
    ba = ba_ref[...]
    bat = bat_ref[0]
    nh = bat.shape[0] // 2
    lane = _iota(ba.shape, 1)
    sub = _iota(bat.shape, 0)

    def col(off):
        return jnp.stack([jnp.sum(jnp.where(lane == g0 + hh + off, ba, 0.0), axis=1, keepdims=True) for hh in range(hb)], axis=0)

    def row(off):
        return jnp.stack([jnp.sum(jnp.where(sub == g0 + hh + off, bat, 0.0), axis=0, keepdims=True) for hh in range(hb)], axis=0)

    braw_c, araw_c, braw_r, araw_r = col(0), col(nh), row(0), row(nh)
    alog = jnp.max(hp_ref[:, pl.ds(0, 1), :], axis=2, keepdims=True)
    dtb = jnp.max(hp_ref[:, pl.ds(1, 1), :], axis=2, keepdims=True)
    nega = -jnp.exp(alog)
    beta_c, beta_r = _sig(braw_c), _sig(braw_r)
    la_c = nega * _softplus(araw_c + dtb)
    la_r = nega * _softplus(araw_r + dtb)
    i = _iota((c, c), 0)
    j = _iota((c, c), 1)
    g_c = jnp.sum(jnp.where(j <= i, la_r, 0.0), axis=2, keepdims=True)
    g_r = jnp.sum(jnp.where(i <= j, la_c, 0.0), axis=1, keepdims=True)
    g_last = jnp.sum(la_c, axis=1, keepdims=True)
    low = i >= j
    dec = jnp.where(low, jnp.exp(jnp.where(low, g_c - g_r, 0.0)), 0.0)
    rq = lax.rsqrt(jnp.sum(qr * qr, axis=2, keepdims=True) + L2_EPS)
    rk = lax.rsqrt(jnp.sum(kr * kr, axis=2, keepdims=True) + L2_EPS)
    q = qr * (rq * HEAD ** -0.5)
    k = kr * rk
    kb = k * beta_c
    lmat = jnp.where(i > j, _bdot(kb, k, BNT) * dec, 0.0)
    eye = jnp.where(i == j, 1.0, 0.0)
    tinv = eye - lmat
    pw = lmat
    for _ in range(int(np.log2(c)) - 1):
        pw = _bdotf(pw, pw, BNN)
        tinv = _bdotf(tinv, eye + pw, BNN)
    eg_c = jnp.exp(g_c)
    rhs_w = kb * eg_c
    sol = _bdotf(tinv, jnp.concatenate([v * beta_c, rhs_w], axis=2), BNN)
    attn = jnp.where(low, _bdot(q, k, BNT) * dec, 0.0)
    ekd = jnp.exp(g_last - g_c)
    return dict(qr=qr, kr=kr, v=v, rq=rq, rk=rk, q=q, k=k, kb=kb, beta_c=beta_c, beta_r=beta_r, la_r=la_r,
                araw_r=araw_r, dtb=dtb, nega=nega, g_c=g_c, g_last=g_last, dec=dec, lmat=lmat, tinv=tinv,
                eg_c=eg_c, rhs_w=rhs_w, sol=sol, u=sol[:, :, :HEAD], w=sol[:, :, HEAD:], attn=attn, q_dec=q * eg_c,
                ekd=ekd, k_dec=k * ekd, i=i, j=j, low=low)


def _gdn_fwd(qkv, ba, bat, hp):
    _, s, d = qkv.shape
    nh, n, c = d // HEAD, s // CHUNK, CHUNK
    hb = min(GDN_HEADS, nh)
    gw = hb * HEAD

    def fn(i_refs, o_refs, scr):
        o_ref, st_ref = o_refs
        s_scr = scr[0]
        st = jnp.where(pl.program_id(1) == 0, 0.0, s_scr[...])
        p = _gdn_pre(*i_refs, hb)
        vn = p["u"] - _bdot(p["w"], st, BNN)
        o = _bdot(p["q_dec"], st, BNN) + _bdot(p["attn"], vn, BNN)
        s_scr[...] = st * jnp.exp(p["g_last"]) + _bdot(p["k_dec"], vn, BTN)
        st_ref[:, 0] = st
        for hh in range(hb):
            o_ref[:, pl.ds(hh * HEAD, HEAD)] = o[hh]

    return _ew("gdn_fwd", fn, (nh // hb, n), [qkv, ba, bat, hp],
               [pl.BlockSpec((3, c, gw), lambda h, t: (0, t, h)), pl.BlockSpec((c, LANE), lambda h, t: (t, 0)),
                pl.BlockSpec((1, bat.shape[1], c), lambda h, t: (t, 0, 0)), pl.BlockSpec((hb, 8, LANE), lambda h, t: (h, 0, 0))],
               [_sds((s, d)), _sds((nh, n, HEAD, HEAD))],
               [pl.BlockSpec((c, gw), lambda h, t: (t, h)), pl.BlockSpec((hb, 1, HEAD, HEAD), lambda h, t: (h, t, 0, 0))],
               scratch=[pltpu.VMEM((hb, HEAD, HEAD), F32)])


def _gdn_bwd(qkv, ba, bat, hp, states, do):
    _, s, d = qkv.shape
    nh, n, c = d // HEAD, s // CHUNK, CHUNK
    hb = min(GDN_HEADS, nh)
    gw = hb * HEAD

    def fn(i_refs, o_refs, scr):
        qkv_ref, ba_ref, bat_ref, hp_ref, st_ref, do_ref = i_refs
        dqkv_ref, dbat_ref, dhp_ref = o_refs
        ds_scr = scr[0]
        first = pl.program_id(1) == 0
        ds1 = jnp.where(first, 0.0, ds_scr[...])
        dhp_old = jnp.where(first, 0.0, dhp_ref[...])
        st = st_ref[:, 0]
        do_v = jnp.stack([do_ref[:, pl.ds(hh * HEAD, HEAD)] for hh in range(hb)], axis=0)
        p = _gdn_pre(qkv_ref, ba_ref, bat_ref, hp_ref, hb)
        i, j, low = p["i"], p["j"], p["low"]
        u, w, sol, attn, dec = p["u"], p["w"], p["sol"], p["attn"], p["dec"]
        q, k, kb, v = p["q"], p["k"], p["kb"], p["v"]
        eg_c, ekd, beta_c = p["eg_c"], p["ekd"], p["beta_c"]
        egl = jnp.exp(p["g_last"])

        def rsum(a):
            return jnp.sum(a, axis=2, keepdims=True)

        def csum(a):
            return jnp.sum(a, axis=1, keepdims=True)

        vn = u - _bdot(w, st, BNN)
        dvn = _bdot(attn, do_v, BTN) + _bdot(p["k_dec"], ds1, BNN)
        dattn = jnp.where(low, _bdot(do_v, vn, BNT), 0.0)
        dqd = _bdot(do_v, st, BNT)
        dkd = _bdot(vn, ds1, BNT)
        ds_scr[...] = _bdot(p["q_dec"], do_v, BTN) + egl * ds1 - _bdot(w, dvn, BTN)
        dgl = csum(rsum(st * ds1)) * egl
        dw = -_bdot(dvn, st, BNT)
        drhs = _bdotf(p["tinv"], jnp.concatenate([dvn, dw], axis=2), BTN)
        da = -jnp.where(i > j, _bdot(drhs, sol, BNT), 0.0)
        mm = da * p["lmat"] + dattn * attn
        dg_c = rsum(mm)
        dg_r = -csum(mm)
        dkk = da * dec
        dqk = dattn * dec
        dkb = _bdot(dkk, k, BNN)
        dk = _bdot(dkk, kb, BTN) + _bdot(dqk, q, BTN)
        dq = _bdot(dqk, k, BNN) + dqd * eg_c
        dg_c = dg_c + rsum(dqd * p["q_dec"])
        dk = dk + dkd * ekd
        t = rsum(dkd * p["k_dec"])
        dgl = dgl + csum(t)
        dg_c = dg_c - t
        drhs_u, drhs_w = drhs[:, :, :HEAD], drhs[:, :, HEAD:]
        dkb = dkb + drhs_w * eg_c
        dg_c = dg_c + rsum(drhs_w * p["rhs_w"])
        dv_out = drhs_u * beta_c
        dbeta_c = rsum(drhs_u * v) + rsum(dkb * k)
        dk = dk + dkb * beta_c
        diag = i == j
        dg = dg_c + rsum(jnp.where(diag, dg_r, 0.0))
        dla_r = csum(jnp.where(low, dg, 0.0)) + dgl
        dbeta_r = csum(jnp.where(diag, dbeta_c, 0.0))
        beta_r = p["beta_r"]
        dbraw_r = dbeta_r * beta_r * (1.0 - beta_r)
        daraw_r = dla_r * p["nega"] * _sig(p["araw_r"] + p["dtb"])
        dalog = rsum(dla_r * p["la_r"])
        ddtb = rsum(daraw_r)
        row8 = _iota((8, c), 0)
        dbat_ref[:, 0] = jnp.where(row8 == 0, dbraw_r, jnp.where(row8 == 1, daraw_r, 0.0))
        rowp = _iota((8, LANE), 0)
        dhp_ref[...] = dhp_old + jnp.where(rowp == 0, dalog, jnp.where(rowp == 1, ddtb, 0.0))
        qr, kr, rq, rk = p["qr"], p["kr"], p["rq"], p["rk"]
        sc = HEAD ** -0.5
        dq_out = sc * (rq * dq - qr * (rq * rq * rq * rsum(dq * qr)))
        dk_out = rk * dk - kr * (rk * rk * rk * rsum(dk * kr))
        for hh in range(hb):
            sl = pl.ds(hh * HEAD, HEAD)
            dqkv_ref[0, :, sl] = dq_out[hh]
            dqkv_ref[1, :, sl] = dk_out[hh]
            dqkv_ref[2, :, sl] = dv_out[hh]

    rev = n - 1
    return _ew("gdn_bwd", fn, (nh // hb, n), [qkv, ba, bat, hp, states, do],
               [pl.BlockSpec((3, c, gw), lambda h, t: (0, rev - t, h)), pl.BlockSpec((c, LANE), lambda h, t: (rev - t, 0)),
                pl.BlockSpec((1, bat.shape[1], c), lambda h, t: (rev - t, 0, 0)), pl.BlockSpec((hb, 8, LANE), lambda h, t: (h, 0, 0)),
                pl.BlockSpec((hb, 1, HEAD, HEAD), lambda h, t: (h, rev - t, 0, 0)), pl.BlockSpec((c, gw), lambda h, t: (rev - t, h))],
               [_sds((3, s, d)), _sds((nh, n, 8, c)), _sds((nh, 8, LANE))],
               [pl.BlockSpec((3, c, gw), lambda h, t: (0, rev - t, h)), pl.BlockSpec((hb, 1, 8, c), lambda h, t: (h, rev - t, 0, 0)),
                pl.BlockSpec((hb, 8, LANE), lambda h, t: (h, 0, 0))],
               scratch=[pltpu.VMEM((hb, HEAD, HEAD), F32)])


def _gdn_post_fwd(o, z, gw, d, zcol):
    s = o.shape[0]
    bs = min(ROW_BLOCK, s)

    def fn(i_refs, o_refs, _):
        o_ref, z_ref, w_ref = i_refs
        wv = w_ref[...]
        for h in range(d // HEAD):
            sl = pl.ds(h * HEAD, HEAD)
            oh, zg = o_ref[:, sl], z_ref[:, sl]
            r = lax.rsqrt(jnp.mean(oh * oh, axis=1, keepdims=True) + L2_EPS)
            o_refs[0][:, sl] = (oh * r * wv * (zg * _sig(zg))).astype(BF16)

    return _ew("gdn_post_fwd", fn, (s // bs,), [o, z, gw], [_rows(bs, d), _rows(bs, d, zcol), _par(1, HEAD)],
               [_sds((s, d), BF16)], [_rows(bs, d)])[0]


def _gdn_post_bwd(dog, o, z, gw, d, zcol):
    s = o.shape[0]
    bs = min(ROW_BLOCK, s)

    def fn(i_refs, o_refs, _):
        g_ref, o_ref, z_ref, w_ref = i_refs
        do_ref, dz_ref, dw_ref = o_refs
        wv = w_ref[...]
        dwacc = jnp.zeros((1, HEAD), F32)
        for h in range(d // HEAD):
            sl = pl.ds(h * HEAD, HEAD)
            oh, zg, gv = o_ref[:, sl], z_ref[:, sl], g_ref[:, sl]
            r = lax.rsqrt(jnp.mean(oh * oh, axis=1, keepdims=True) + L2_EPS)
            on = oh * r
            sil = zg * _sig(zg)
            dz_ref[:, sl] = (gv * on * wv * _dsilu(zg)).astype(BF16)
            dwacc = dwacc + _colsum(gv * on * sil)
            don = gv * wv * sil
            do_ref[:, sl] = r * (don - on * jnp.mean(don * on, axis=1, keepdims=True))
        _accum(dw_ref, dwacc, pl.program_id(0) == 0)

    return _ew("gdn_post_bwd", fn, (s // bs,), [dog, o, z, gw],
               [_rows(bs, d), _rows(bs, d), _rows(bs, d, zcol), _par(1, HEAD)],
               [_sds((s, d)), _sds((s, d), BF16), _sds((1, HEAD))], [_rows(bs, d), _rows(bs, d), _par(1, HEAD)])


def _merge_fwd(z, yc, yg, d, col_a, col_b):
    s = z.shape[0]
    bs = min(ROW_BLOCK, s)

    def fn(i_refs, o_refs, _):
        ga, gb, yc_ref, yg_ref = i_refs
        o_refs[0][...] = (_sig(ga[...]) * yc_ref[...] + _sig(gb[...]) * yg_ref[...]).astype(BF16)

    return _ew("merge_fwd", fn, (s // bs,), [z, z, yc, yg],
               [_rows(bs, d, col_a), _rows(bs, d, col_b), _rows(bs, d), _rows(bs, d)], [_sds((s, d), BF16)], [_rows(bs, d)])[0]


def _merge_bwd(dm, z, yc, yg, d, col_a, col_b):
    s = z.shape[0]
    bs = min(ROW_BLOCK, s)

    def fn(i_refs, o_refs, _):
        dm_ref, ga, gb, yc_ref, yg_ref = i_refs
        dyc_ref, dyg_ref, dga_ref, dgb_ref, dbc_ref = o_refs
        dmv = dm_ref[...]
        sa, sb = _sig(ga[...]), _sig(gb[...])
        dyc = dmv * sa
        dyc_ref[...] = dyc.astype(BF16)
        dyg_ref[...] = (dmv * sb).astype(BF16)
        dga_ref[...] = (dmv * yc_ref[...] * sa * (1.0 - sa)).astype(BF16)
        dgb_ref[...] = (dmv * yg_ref[...] * sb * (1.0 - sb)).astype(BF16)
        _accum(dbc_ref, _colsum(dyc), pl.program_id(0) == 0)

    return _ew("merge_bwd", fn, (s // bs,), [dm, z, z, yc, yg],
               [_rows(bs, d), _rows(bs, d, col_a), _rows(bs, d, col_b), _rows(bs, d), _rows(bs, d)],
               [_sds((s, d), BF16)] * 4 + [_sds((1, d))], [_rows(bs, d)] * 4 + [_par(1, d)])


def _swiglu_fwd(hg, hu):
    s, f = hg.shape
    bs = min(ROW_BLOCK, s)
    cw = _tile(f, 2048)

    def fn(i_refs, o_refs, _):
        g = i_refs[0][...]
        o_refs[0][...] = (g * _sig(g) * i_refs[1][...]).astype(BF16)

    spec = pl.BlockSpec((bs, cw), lambda i, j: (i, j))
    return _ew("swiglu_fwd", fn, (s // bs, f // cw), [hg, hu], [spec, spec], [_sds((s, f), BF16)], [spec])[0]


def _swiglu_bwd(hg, hu, df):
    s, f = hg.shape
    bs = min(ROW_BLOCK, s)
    cw = _tile(f, 2048)

    def fn(i_refs, o_refs, _):
        g, u, dfv = i_refs[0][...], i_refs[1][...], i_refs[2][...]
        o_refs[0][...] = (dfv * u * _dsilu(g)).astype(BF16)
        o_refs[1][...] = (dfv * g * _sig(g)).astype(BF16)

    spec = pl.BlockSpec((bs, cw), lambda i, j: (i, j))
    return _ew("swiglu_bwd", fn, (s // bs, f // cw), [hg, hu, df], [spec] * 3, [_sds((s, f), BF16)] * 2, [spec] * 2)


def _col_sums(name, a):
    s, n = a.shape
    bs = min(ROW_BLOCK, s)
    cw = _tile(n, 2048)

    def fn(i_refs, o_refs, _):
        _accum(o_refs[0], _colsum(i_refs[0][...].astype(F32)), pl.program_id(1) == 0)

    return _ew(name, fn, (n // cw, s // bs), [a], [pl.BlockSpec((bs, cw), lambda j, i: (i, j))],
               [_sds((1, n))], [pl.BlockSpec((1, cw), lambda j, i: (0, j))])[0]


def _adamw(name, w, g, m, v):
    r, c = w.shape
    br = r
    if r * c * 4 > (1 << 20):
        cands = [t for t in range(8, r, 8) if r % t == 0 and t * c * 4 <= (1 << 20)]
        br = max(cands) if cands else r
    c1 = 1.0 - ADAM_B1 ** ADAM_STEP
    c2 = 1.0 - ADAM_B2 ** ADAM_STEP

    def fn(i_refs, o_refs, _):
        wv, gv, mv, vv = (x[...] for x in i_refs)
        m2 = ADAM_B1 * mv + (1.0 - ADAM_B1) * gv
        v2 = ADAM_B2 * vv + (1.0 - ADAM_B2) * (gv * gv)
        o_refs[0][...] = -ADAM_LR * ((m2 / c1) / (jnp.sqrt(v2 / c2) + ADAM_EPS) + ADAM_WD * wv)
        o_refs[1][...] = m2
        o_refs[2][...] = v2

    spec = pl.BlockSpec((br, c), lambda i: (i, 0))
    return _ew(name, fn, (r // br,), [w, g, m, v], [spec] * 4, [_sds((r, c))] * 3, [spec] * 3)


ANY = pl.BlockSpec(memory_space=pl.ANY)


def _place():
    return lax.axis_index("x"), lax.axis_index("y"), lax.axis_index("c")


def _gather_chips(flat):
    r, w = flat.shape

    def body(x_ref, o_ref, ssem, rsem, lsem):
        x, y, c = _place()
        me = 2 * x + y
        chips = [(1 - x, y), (x, 1 - y), (1 - x, 1 - y)]
        local = pltpu.make_async_copy(x_ref, o_ref.at[me], lsem)
        local.start()
        sends = []
        for j, (px, py) in enumerate(chips):
            cp = pltpu.make_async_remote_copy(src_ref=x_ref, dst_ref=o_ref.at[me], send_sem=ssem.at[j], recv_sem=rsem.at[j],
                                              device_id=(px, py, c), device_id_type=MESH)
            cp.start()
            sends.append(cp)
        for j, (px, py) in enumerate(chips):
            pltpu.make_async_remote_copy(src_ref=x_ref, dst_ref=o_ref.at[2 * px + py], send_sem=ssem.at[j], recv_sem=rsem.at[j],
                                         device_id=(px, py, c), device_id_type=MESH).wait_recv()
        for cp in sends:
            cp.wait_send()
        local.wait()

    return pl.pallas_call(
        body, name="gather_chips", in_specs=[ANY], out_specs=ANY, out_shape=_sds((4, r, w), flat.dtype),
        scratch_shapes=[pltpu.SemaphoreType.DMA((3,)), pltpu.SemaphoreType.DMA((3,)), pltpu.SemaphoreType.DMA(())],
    )(flat)


def _gather_all(flat):
    r, w = flat.shape
    masks = [(mx, my, mc) for mx in (0, 1) for my in (0, 1) for mc in (0, 1)][1:]

    def body(x_ref, o_ref, ssem, rsem, lsem):
        x, y, c = _place()
        me = 4 * x + 2 * y + c
        local = pltpu.make_async_copy(x_ref, o_ref.at[me], lsem)
        local.start()
        peers = [(x ^ mx, y ^ my, c ^ mc) for (mx, my, mc) in masks]
        sends = []
        for j, peer in enumerate(peers):
            cp = pltpu.make_async_remote_copy(src_ref=x_ref, dst_ref=o_ref.at[me], send_sem=ssem.at[j], recv_sem=rsem.at[j],
                                              device_id=peer, device_id_type=MESH)
            cp.start()
            sends.append(cp)
        for j, (px, py, pc) in enumerate(peers):
            pltpu.make_async_remote_copy(src_ref=x_ref, dst_ref=o_ref.at[4 * px + 2 * py + pc], send_sem=ssem.at[j],
                                         recv_sem=rsem.at[j], device_id=(px, py, pc), device_id_type=MESH).wait_recv()
        for cp in sends:
            cp.wait_send()
        local.wait()

    return pl.pallas_call(
        body, name="gather_all", in_specs=[ANY], out_specs=ANY, out_shape=_sds((8, r, w), flat.dtype),
        scratch_shapes=[pltpu.SemaphoreType.DMA((7,)), pltpu.SemaphoreType.DMA((7,)), pltpu.SemaphoreType.DMA(())],
    )(flat)


def _sum8(parts):
    _, r, w = parts.shape

    def fn(i_refs, o_refs, _):
        acc = i_refs[0][0]
        for k in range(1, 8):
            acc = acc + i_refs[0][k]
        o_refs[0][...] = acc

    br = _tile(r, 512, 8)
    return _ew("sum8", fn, (r // br,), [parts], [pl.BlockSpec((8, br, w), lambda i: (0, i, 0))], [_sds((r, w))],
               [pl.BlockSpec((br, w), lambda i: (i, 0))])[0]


def _swap_halves(g):
    _, r, w = g.shape
    rh = r // 2

    def body(g_ref, o_ref, ssem, rsem):
        x, y, c = _place()
        cp = pltpu.make_async_remote_copy(src_ref=g_ref.at[:, pl.ds((1 - c) * rh, rh), :], dst_ref=o_ref, send_sem=ssem,
                                          recv_sem=rsem, device_id=(x, y, 1 - c), device_id_type=MESH)
        cp.start()
        cp.wait()

    return pl.pallas_call(
        body, name="swap_halves", in_specs=[ANY], out_specs=ANY, out_shape=_sds((4, rh, w), g.dtype),
        scratch_shapes=[pltpu.SemaphoreType.DMA(()), pltpu.SemaphoreType.DMA(())],
    )(g)


def _add_half(g, got, c_arr):
    _, r, w = g.shape
    rh = r // 2
    br = _tile(rh, 512, 16)
    nb = rh // br

    def body(c_ref, g_ref, r_ref, o_ref):
        o_ref[...] = (g_ref[...] + r_ref[...]).astype(BF16)

    return pl.pallas_call(
        body, name="add_half",
        grid_spec=pltpu.PrefetchScalarGridSpec(
            num_scalar_prefetch=1, grid=(4, nb),
            in_specs=[pl.BlockSpec((1, br, w), lambda s, i, c_ref: (s, c_ref[0] * nb + i, 0)),
                      pl.BlockSpec((1, br, w), lambda s, i, c_ref: (s, i, 0))],
            out_specs=pl.BlockSpec((1, br, w), lambda s, i, c_ref: (s, i, 0))),
        out_shape=_sds((4, rh, w), BF16), compiler_params=_params(2),
    )(c_arr, g, got)


def _scatter_chips(p):
    _, rh, w = p.shape

    def body(p_ref, o_ref, ssem, rsem):
        x, y, c = _place()
        chips = [(1 - x, y), (x, 1 - y), (1 - x, 1 - y)]
        sends = []
        for j, (px, py) in enumerate(chips):
            cp = pltpu.make_async_remote_copy(src_ref=p_ref.at[2 * px + py], dst_ref=o_ref.at[j], send_sem=ssem.at[j],
                                              recv_sem=rsem.at[j], device_id=(px, py, c), device_id_type=MESH)
            cp.start()
            sends.append(cp)
        for cp in sends:
            cp.wait_recv()
        for cp in sends:
            cp.wait_send()

    return pl.pallas_call(
        body, name="scatter_chips", in_specs=[ANY], out_specs=ANY, out_shape=_sds((3, rh, w), p.dtype),
        scratch_shapes=[pltpu.SemaphoreType.DMA((3,)), pltpu.SemaphoreType.DMA((3,))],
    )(p)


def _add_chips(p, got, idx_arr):
    _, rh, w = p.shape
    br = _tile(rh, 512, 16)
    nb = rh // br

    def body(idx_ref, p_ref, r_ref, o_ref):
        o_ref[...] = ((p_ref[0].astype(F32) + r_ref[0].astype(F32)) + r_ref[1].astype(F32)) + r_ref[2].astype(F32)

    return pl.pallas_call(
        body, name="add_chips",
        grid_spec=pltpu.PrefetchScalarGridSpec(
            num_scalar_prefetch=1, grid=(nb,),
            in_specs=[pl.BlockSpec((1, br, w), lambda i, idx: (idx[0], i, 0)),
                      pl.BlockSpec((3, br, w), lambda i, idx: (0, i, 0))],
            out_specs=pl.BlockSpec((br, w), lambda i, idx: (idx[1] * nb + i, 0))),
        out_shape=_sds((2 * rh, w)), compiler_params=_params(1),
    )(idx_arr, p, got)


def _join_halves(buf):
    r, w = buf.shape
    rh = r // 2

    def body(b_ref, o_ref, ssem, rsem):
        x, y, c = _place()
        mine = o_ref.at[pl.ds(c * rh, rh), :]
        cp = pltpu.make_async_remote_copy(src_ref=mine, dst_ref=mine, send_sem=ssem, recv_sem=rsem,
                                          device_id=(x, y, 1 - c), device_id_type=MESH)
        cp.start()
        other = o_ref.at[pl.ds((1 - c) * rh, rh), :]
        pltpu.make_async_remote_copy(src_ref=other, dst_ref=other, send_sem=ssem, recv_sem=rsem,
                                     device_id=(x, y, 1 - c), device_id_type=MESH).wait_recv()
        cp.wait_send()

    return pl.pallas_call(
        body, name="join_halves", in_specs=[ANY], out_specs=ANY, out_shape=_sds((r, w), buf.dtype),
        input_output_aliases={0: 0},
        scratch_shapes=[pltpu.SemaphoreType.DMA(()), pltpu.SemaphoreType.DMA(())],
    )(buf)


def _gather_split(flat):
    r, w = flat.shape
    rh = r // 2

    def body(x_ref, o_ref, ssem, rsem):
        x, y, c = _place()
        me = 2 * x + y
        chips = [(1 - x, y), (x, 1 - y), (1 - x, 1 - y)]
        sib = (x, y, 1 - c)
        mine = pl.ds(c * rh, rh)
        other = pl.ds((1 - c) * rh, rh)
        sends = []
        for j, (px, py) in enumerate(chips):
            cp = pltpu.make_async_remote_copy(src_ref=x_ref.at[mine, :], dst_ref=o_ref.at[me, mine, :], send_sem=ssem.at[j],
                                              recv_sem=rsem.at[j], device_id=(px, py, c), device_id_type=MESH)
            cp.start()
            sends.append(cp)
        for j, (px, py) in enumerate(chips):
            got = o_ref.at[2 * px + py, mine, :]
            pltpu.make_async_remote_copy(src_ref=x_ref.at[mine, :], dst_ref=got, send_sem=ssem.at[j], recv_sem=rsem.at[j],
                                         device_id=(px, py, c), device_id_type=MESH).wait_recv()
            cp = pltpu.make_async_remote_copy(src_ref=got, dst_ref=got, send_sem=ssem.at[3 + j], recv_sem=rsem.at[3 + j],
                                              device_id=sib, device_id_type=MESH)
            cp.start()
            sends.append(cp)
        for j, (px, py) in enumerate(chips):
            theirs = o_ref.at[2 * px + py, other, :]
            pltpu.make_async_remote_copy(src_ref=theirs, dst_ref=theirs, send_sem=ssem.at[3 + j], recv_sem=rsem.at[3 + j],
                                         device_id=sib, device_id_type=MESH).wait_recv()
        for cp in sends:
            cp.wait_send()

    return pl.pallas_call(
        body, name="gather_split", in_specs=[ANY], out_specs=ANY, out_shape=_sds((4, r, w), flat.dtype),
        scratch_shapes=[pltpu.SemaphoreType.DMA((6,)), pltpu.SemaphoreType.DMA((6,))],
    )(flat)


BIG = ("w_in", "w_conv_proj", "w_gdn_proj", "w_out", "w_ffn_in", "w_ffn_out")
COL_SHARDED = ("w_in", "w_ffn_in")
SMALL = ("b_in", "conv_dw_w", "conv_dw_b", "conv_ln_g", "conv_ln_b", "b_conv_proj", "short_conv_w", "a_log", "dt_bias",
         "gdn_norm_w", "ln1_g", "ln1_b", "ln2_g", "ln2_b")
SMALL_SHARDED = ("conv_dw_w", "short_conv_w")
ORDER = ("w_in", "b_in", "conv_dw_w", "conv_dw_b", "conv_ln_g", "conv_ln_b", "w_conv_proj", "b_conv_proj", "short_conv_w",
         "a_log", "dt_bias", "gdn_norm_w", "w_gdn_proj", "w_out", "ln1_g", "ln1_b", "w_ffn_in", "w_ffn_out", "ln2_g", "ln2_b")


def _flat_rows(shapes, fw):
    rows = [int(np.prod(sh)) // fw for sh in shapes]
    total = sum(rows)
    pad = (-total) % (2048 if total >= 4096 else 32)
    return rows, total + pad, pad


def _row(v):
    return v.reshape(1, -1)


def _layer_fwd(x, xb, wl, alpha):
    d = x.shape[1]
    z = _mm("mm_z", xb, wl["w_main"], "nn", bias=wl["b_main"])
    ba = _mm("mm_ba", xb, wl["w_ba"], "nn", bias=wl["b_ba"])
    c3 = _conv_fwd(z, wl["conv_dw_w"], wl["conv_dw_b"], wl["conv_ln_g"], wl["conv_ln_b"], d)
    yc = _mm("mm_proj", c3, wl["w_conv_proj"], "nn", bias=wl["b_conv_proj"])
    qkv = _sconv_fwd(z, wl["short_conv_w"], d, 2)
    s = x.shape[0]
    nh = d // HEAD
    bat = jnp.transpose(ba[:, :2 * nh].reshape(s // CHUNK, CHUNK, 2 * nh), (0, 2, 1))
    o, states = _gdn_fwd(qkv, ba, bat, wl["hp"])
    og = _gdn_post_fwd(o, z, wl["gdn_norm_w"], d, 5)
    yg = _mm("mm_proj", og, wl["w_gdn_proj"], "nn", bias=wl["zero_bias"])
    m = _merge_fwd(z, yc, yg, d, 6, 7)
    mix = _mm("mm_proj", m, wl["w_out"], "nn", bias=wl["zero_bias"])
    x1, x1b = _ln_res_fwd(x, mix, wl["ln1_g"], wl["ln1_b"], alpha)
    hg = _mm("mm_ffn_in", x1b, wl["w_ffn_g"], "nn")
    hu = _mm("mm_ffn_in", x1b, wl["w_ffn_u"], "nn")
    f = _swiglu_fwd(hg, hu)
    ffn = _mm("mm_ffn_out", f, wl["w_ffn_out"], "nn")
    x2, x2b = _ln_res_fwd(x1, ffn, wl["ln2_g"], wl["ln2_b"], alpha)
    saved = dict(x=x, xb=xb, x1b=x1b, z=z, ba=ba, bat=bat, c3=c3, yc=yc, qkv=qkv, o=o, states=states, og=og, yg=yg, m=m, mix=mix, x1=x1,
                 hg=hg, hu=hu, f=f, ffn=ffn)
    return x2, x2b, saved


def _layer_bwd(dy, sv, wl, alpha):
    x, z, x1, xb, x1b = sv["x"], sv["z"], sv["x1"], sv["xb"], sv["x1b"]
    s, d = x.shape
    nh = d // HEAD
    g = {}
    dr2, dr2b, g["ln2_g"], g["ln2_b"] = _ln_res_bwd(x1, sv["ffn"], wl["ln2_g"], dy, alpha)
    df = _mm("mm_dffn_out", dr2b, wl["w_ffn_out"], "nt")
    g["w_ffn_out"] = _mm("mm_gw_ffn_out", sv["f"], dr2b, "tn")
    dhg, dhu = _swiglu_bwd(sv["hg"], sv["hu"], df)
    t = _mm("mm_dffn_in", dhg, wl["w_ffn_g"], "nt", res=dr2, res_scale=alpha)
    dx1 = _mm("mm_dffn_in", dhu, wl["w_ffn_u"], "nt", res=t)
    g["w_ffn_in"] = jnp.concatenate([_mm("mm_gw_ffn_in", x1b, dhg, "tn"), _mm("mm_gw_ffn_in", x1b, dhu, "tn")], axis=1)
    dr1, dr1b, g["ln1_g"], g["ln1_b"] = _ln_res_bwd(x, sv["mix"], wl["ln1_g"], dx1, alpha)
    dm = _mm("mm_dproj", dr1b, wl["w_out"], "nt")
    g["w_out"] = _mm("mm_gw_proj", sv["m"], dr1b, "tn")
    dyc, dyg, dga, dgb, g["b_conv_proj"] = _merge_bwd(dm, z, sv["yc"], sv["yg"], d, 6, 7)
    dc3 = _mm("mm_dproj", dyc, wl["w_conv_proj"], "nt")
    g["w_conv_proj"] = _mm("mm_gw_proj", sv["c3"], dyc, "tn")
    dog = _mm("mm_dproj", dyg, wl["w_gdn_proj"], "nt")
    g["w_gdn_proj"] = _mm("mm_gw_proj", sv["og"], dyg, "tn")
    dglu_a, dglu_b, g["conv_dw_w"], g["conv_dw_b"], g["conv_ln_g"], g["conv_ln_b"] = _conv_bwd(
        z, dc3, wl["conv_dw_w"], wl["conv_dw_b"], wl["conv_ln_g"], wl["conv_ln_b"], d)
    do, dzg, g["gdn_norm_w"] = _gdn_post_bwd(dog, sv["o"], z, wl["gdn_norm_w"], d, 5)
    dqkv, dbat, dhp = _gdn_bwd(sv["qkv"], sv["ba"], sv["bat"], wl["hp"], sv["states"], do)
    g["a_log"] = dhp[:, 0, 0]
    g["dt_bias"] = dhp[:, 1, 0]
    dzq, g["short_conv_w"] = _sconv_bwd(z, dqkv, wl["short_conv_w"], d, 2)
    dz = jnp.concatenate([dglu_a, dglu_b, dzq[0], dzq[1], dzq[2], dzg, dga, dgb], axis=1)
    dba = jnp.transpose(dbat[:, :, :2, :], (1, 3, 2, 0)).reshape(s, 2 * nh)
    dba = jnp.pad(dba, ((0, 0), (0, LANE - 2 * nh)))
    t = _mm("mm_dba", dba, wl["w_ba"], "nt", res=dr1, res_scale=alpha)
    dx = _mm("mm_dz", dz, wl["w_main"], "nt", res=t)
    gw_main = _mm("mm_gw_main", xb, dz, "tn")
    gw_ba = _mm("mm_gw_ba", xb, dba, "tn")
    g["w_in"] = jnp.concatenate([gw_main[:, :6 * d], gw_ba[:, :2 * nh], gw_main[:, 6 * d:]], axis=1)
    dbz = _col_sums("colsum_dz", dz)
    dbb = _col_sums("colsum_dba", dba)
    g["b_in"] = jnp.concatenate([dbz[:, :6 * d], dbb[:, :2 * nh], dbz[:, 6 * d:]], axis=1)
    return dx, g


def kernel(x, w_in, b_in, conv_dw_w, conv_dw_b, conv_ln_g, conv_ln_b, w_conv_proj, b_conv_proj, short_conv_w, a_log, dt_bias, gdn_norm_w, w_gdn_proj, w_out, ln1_g, ln1_b, w_ffn_in, w_ffn_out, ln2_g, ln2_b, loss_target, m_w_in, m_b_in, m_conv_dw_w, m_conv_dw_b, m_conv_ln_g, m_conv_ln_b, m_w_conv_proj, m_b_conv_proj, m_short_conv_w, m_a_log, m_dt_bias, m_gdn_norm_w, m_w_gdn_proj, m_w_out, m_ln1_g, m_ln1_b, m_w_ffn_in, m_w_ffn_out, m_ln2_g, m_ln2_b, v_w_in, v_b_in, v_conv_dw_w, v_conv_dw_b, v_conv_ln_g, v_conv_ln_b, v_w_conv_proj, v_b_conv_proj, v_short_conv_w, v_a_log, v_dt_bias, v_gdn_norm_w, v_w_gdn_proj, v_w_out, v_ln1_g, v_ln1_b, v_w_ffn_in, v_w_ffn_out, v_ln2_g, v_ln2_b):
    wts = dict(w_in=w_in, b_in=b_in, conv_dw_w=conv_dw_w, conv_dw_b=conv_dw_b, conv_ln_g=conv_ln_g, conv_ln_b=conv_ln_b,
               w_conv_proj=w_conv_proj, b_conv_proj=b_conv_proj, short_conv_w=short_conv_w, a_log=a_log, dt_bias=dt_bias,
               gdn_norm_w=gdn_norm_w, w_gdn_proj=w_gdn_proj, w_out=w_out, ln1_g=ln1_g, ln1_b=ln1_b, w_ffn_in=w_ffn_in,
               w_ffn_out=w_ffn_out, ln2_g=ln2_g, ln2_b=ln2_b)
    mom = dict(w_in=m_w_in, b_in=m_b_in, conv_dw_w=m_conv_dw_w, conv_dw_b=m_conv_dw_b, conv_ln_g=m_conv_ln_g,
               conv_ln_b=m_conv_ln_b, w_conv_proj=m_w_conv_proj, b_conv_proj=m_b_conv_proj, short_conv_w=m_short_conv_w,
               a_log=m_a_log, dt_bias=m_dt_bias, gdn_norm_w=m_gdn_norm_w, w_gdn_proj=m_w_gdn_proj, w_out=m_w_out,
               ln1_g=m_ln1_g, ln1_b=m_ln1_b, w_ffn_in=m_w_ffn_in, w_ffn_out=m_w_ffn_out, ln2_g=m_ln2_g, ln2_b=m_ln2_b)
    var = dict(w_in=v_w_in, b_in=v_b_in, conv_dw_w=v_conv_dw_w, conv_dw_b=v_conv_dw_b, conv_ln_g=v_conv_ln_g,
               conv_ln_b=v_conv_ln_b, w_conv_proj=v_w_conv_proj, b_conv_proj=v_b_conv_proj, short_conv_w=v_short_conv_w,
               a_log=v_a_log, dt_bias=v_dt_bias, gdn_norm_w=v_gdn_norm_w, w_gdn_proj=v_w_gdn_proj, w_out=v_w_out,
               ln1_g=v_ln1_g, ln1_b=v_ln1_b, w_ffn_in=v_w_ffn_in, w_ffn_out=v_w_ffn_out, ln2_g=v_ln2_g, ln2_b=v_ln2_b)

    depth = w_in.shape[0]
    _, s, d = x.shape
    nh = d // HEAD
    alpha = float((2.0 * depth) ** 0.25)
    xi, yi, ci = _place()
    chip = 2 * xi + yi
    c_arr = jnp.reshape(ci, (1,)).astype(jnp.int32)
    idx_arr = jnp.stack([chip, ci]).astype(jnp.int32)

    shard_shapes = [wts[n].shape[1:] for n in BIG]
    fw = FLAT_W if all(int(np.prod(sh)) % FLAT_W == 0 for sh in shard_shapes) else LANE
    rows, total_rows, pad_rows = _flat_rows(shard_shapes, fw)
    layers = []
    for l in range(depth):
        parts = [wts[n][l].reshape(-1, fw) for n in BIG]
        if pad_rows:
            parts.append(jnp.zeros((pad_rows, fw), F32))
        mine = jnp.concatenate(parts, axis=0).astype(BF16)
        gathered = lax.dynamic_update_slice(_gather_split(mine), mine[None], (chip, 0, 0))
        full, off = {}, 0
        for n, sh, nr in zip(BIG, shard_shapes, rows):
            blk = gathered[:, off:off + nr, :].reshape((4,) + tuple(sh))
            off += nr
            if n in COL_SHARDED:
                full[n] = jnp.transpose(blk, (1, 0, 2)).reshape(sh[0], 4 * sh[1])
            else:
                full[n] = blk.reshape(4 * sh[0], sh[1])
        wi = full["w_in"]
        bi = b_in[l]
        f_ff = full["w_ffn_in"].shape[1] // 2
        layers.append(dict(
            w_main=jnp.concatenate([wi[:, :6 * d], wi[:, 6 * d + 2 * nh:]], axis=1),
            w_ba=jnp.pad(wi[:, 6 * d:6 * d + 2 * nh], ((0, 0), (0, LANE - 2 * nh))),
            b_main=_row(jnp.concatenate([bi[:6 * d], bi[6 * d + 2 * nh:]])),
            b_ba=_row(jnp.pad(bi[6 * d:6 * d + 2 * nh], (0, LANE - 2 * nh))),
            w_conv_proj=full["w_conv_proj"], w_gdn_proj=full["w_gdn_proj"], w_out=full["w_out"],
            w_ffn_g=full["w_ffn_in"][:, :f_ff], w_ffn_u=full["w_ffn_in"][:, f_ff:], w_ffn_out=full["w_ffn_out"],
            b_conv_proj=_row(b_conv_proj[l]), zero_bias=jnp.zeros((1, d), F32),
            conv_dw_b=_row(conv_dw_b[l]), conv_ln_g=_row(conv_ln_g[l]), conv_ln_b=_row(conv_ln_b[l]),
            gdn_norm_w=_row(gdn_norm_w[l]), ln1_g=_row(ln1_g[l]), ln1_b=_row(ln1_b[l]), ln2_g=_row(ln2_g[l]), ln2_b=_row(ln2_b[l]),
            hp=jnp.concatenate([jnp.broadcast_to(a_log[l][:, None, None], (nh, 1, LANE)),
                                jnp.broadcast_to(dt_bias[l][:, None, None], (nh, 1, LANE)),
                                jnp.zeros((nh, 6, LANE), F32)], axis=1),
        ))

    kw, cs = conv_dw_w.shape[1], conv_dw_w.shape[2]
    ks, ss = short_conv_w.shape[1], short_conv_w.shape[2]
    small_w = jnp.concatenate([conv_dw_w.reshape(depth * kw, cs), jnp.zeros(((-depth * kw) % 8, cs), F32)], axis=0)
    cw_all = _gather_chips(small_w)[:, :depth * kw].reshape(4, depth, kw, cs)
    cw_all = jnp.transpose(cw_all, (1, 2, 0, 3)).reshape(depth, kw, 4 * cs)
    small_s = jnp.concatenate([short_conv_w.reshape(depth * ks, ss), jnp.zeros(((-depth * ks) % 8, ss), F32)], axis=0)
    sw_all = _gather_chips(small_s)[:, :depth * ks].reshape(4, depth, ks, ss)
    sw_all = jnp.transpose(sw_all, (1, 2, 0, 3)).reshape(depth, ks, 4 * ss)
    for l in range(depth):
        layers[l]["conv_dw_w"] = cw_all[l]
        layers[l]["short_conv_w"] = sw_all[l]

    h = x[0]
    hb16 = h.astype(BF16)
    saved = []
    for l in range(depth):
        h, hb16, sv = _layer_fwd(h, hb16, layers[l], alpha)
        saved.append(sv)
    loss_blk, dy = _loss_fwd_bwd(h, loss_target[0])
    loss = lax.psum(0.5 * loss_blk[0, 0], ("x", "y", "c"))

    grads = [None] * depth
    shard_grads = [None] * depth
    for l in reversed(range(depth)):
        dy, g = _layer_bwd(dy, saved[l], layers[l], alpha)
        grads[l] = g
        parts = []
        for n, sh in zip(BIG, shard_shapes):
            gf = g[n]
            if n in COL_SHARDED:
                gs = jnp.transpose(gf.reshape(sh[0], 4, sh[1]), (1, 0, 2))
            else:
                gs = gf.reshape(4, sh[0], sh[1])
            parts.append(gs.reshape(4, -1, fw))
        if pad_rows:
            parts.append(jnp.zeros((4, pad_rows, fw), F32))
        gflat = jnp.concatenate(parts, axis=1)
        pair = _add_half(gflat, _swap_halves(gflat), c_arr)
        shard = _join_halves(_add_chips(pair, _scatter_chips(pair), idx_arr))
        sg, off = {}, 0
        for n, sh, nr in zip(BIG, shard_shapes, rows):
            sg[n] = shard[off:off + nr].reshape(sh)
            off += nr
        shard_grads[l] = sg
    grad_x = dy[None]

    small_parts = [jnp.concatenate([grads[l][n].reshape(-1) for l in range(depth)]) for n in SMALL]
    sizes = [int(p.shape[0]) for p in small_parts]
    flat = jnp.concatenate(small_parts)
    nflat = int(flat.shape[0])
    flat = jnp.pad(flat, (0, (-nflat) % (8 * LANE))).reshape(-1, LANE)
    tot = _sum8(_gather_all(flat)).reshape(-1)
    small_g, off = {}, 0
    for n, sz in zip(SMALL, sizes):
        full = tot[off:off + sz]
        off += sz
        if n == "conv_dw_w":
            full = lax.dynamic_slice_in_dim(full.reshape(depth, kw, 4 * cs), chip * cs, cs, axis=2)
        elif n == "short_conv_w":
            full = lax.dynamic_slice_in_dim(full.reshape(depth, ks, 4 * ss), chip * ss, ss, axis=2)
        small_g[n] = full.reshape(wts[n].shape)

    out_g, out_d, out_m, out_v = {}, {}, {}, {}
    for n in BIG:
        gfull = jnp.stack([shard_grads[l][n] for l in range(depth)])
        sh = wts[n].shape
        two = (sh[0] * sh[1], sh[2])
        dl, nm, nv = _adamw("adamw_" + n, wts[n].reshape(two), gfull.reshape(two), mom[n].reshape(two), var[n].reshape(two))
        out_g[n], out_d[n], out_m[n], out_v[n] = gfull, dl.reshape(sh), nm.reshape(sh), nv.reshape(sh)

    def pack(src):
        v = jnp.concatenate([src[n].reshape(-1) for n in SMALL])
        return jnp.pad(v, (0, (-int(v.shape[0])) % (8 * LANE))).reshape(-1, LANE)

    dl, nm, nv = _adamw("adamw_small", pack(wts), pack(small_g), pack(mom), pack(var))
    off = 0
    for n in SMALL:
        sz = int(np.prod(wts[n].shape))
        out_g[n] = small_g[n]
        out_d[n] = dl.reshape(-1)[off:off + sz].reshape(wts[n].shape)
        out_m[n] = nm.reshape(-1)[off:off + sz].reshape(wts[n].shape)
        out_v[n] = nv.reshape(-1)[off:off + sz].reshape(wts[n].shape)
        off += sz

    return (loss, grad_x, *[out_g[n] for n in ORDER], *[out_d[n] for n in ORDER], *[out_m[n] for n in ORDER],
            *[out_v[n] for n in ORDER])
```

```python
import functools

import jax
import jax.numpy as jnp
import numpy as np
from jax import lax
from jax.experimental import pallas as pl
from jax.experimental.pallas import tpu as pltpu

F32 = jnp.float32
BF16 = jnp.bfloat16
MESH = pl.DeviceIdType.MESH

LN_EPS = 1e-5
L2_EPS = 1e-6
CHUNK = 64
HEAD = 128
CONV_HALO = 32
SHORT_HALO = 8
ROW_BLOCK = 256
CONV_ROWS = 64
VMEM_LIMIT = 56 * 1024 * 1024
LANE = 128
FLAT_W = 1024

ADAM_LR, ADAM_B1, ADAM_B2, ADAM_EPS, ADAM_WD, ADAM_STEP = 0.001, 0.9, 0.999, 1e-08, 0.01, 10

NN = ((1,), (0,))
NT = ((1,), (1,))
TN = ((0,), (0,))


def _params(n_axes):
    return pltpu.CompilerParams(dimension_semantics=("arbitrary",) * n_axes, vmem_limit_bytes=VMEM_LIMIT)


def _tile(dim, pref, unit=LANE):
    if dim <= pref:
        return dim
    best = None
    for t in range(unit, pref + 1, unit):
        if dim % t == 0:
            best = t
    assert best is not None, (dim, pref, unit)
    return best


def _dotb(a, b, dims):
    return lax.dot_general(a.astype(BF16), b.astype(BF16), (dims, ((), ())), preferred_element_type=F32)


def _split(a):
    hi = a.astype(BF16)
    return hi, (a - hi.astype(F32)).astype(BF16)


def _sig(x):
    return jax.nn.sigmoid(x)


def _dsilu(x):
    s = _sig(x)
    return s * (1.0 + x * (1.0 - s))


def _softplus(x):
    return jnp.maximum(x, 0.0) + jnp.log(1.0 + jnp.exp(-jnp.abs(x)))


def _iota(shape, dim):
    return lax.broadcasted_iota(jnp.int32, shape, dim)


def _accum(ref, val, first):
    @pl.when(first)
    def _():
        ref[...] = val

    @pl.when(jnp.logical_not(first))
    def _():
        ref[...] += val


def _mm(name, a, b, mode, out_dtype=F32, bias=None, res=None, res_scale=1.0, tm=1024, tn=1024, tk=2048):
    if mode == "nn":
        (m, k), (k2, n) = a.shape, b.shape
    elif mode == "tn":
        (k, m), (k2, n) = a.shape, b.shape
    else:
        (m, k), (n, k2) = a.shape, b.shape
    assert k == k2, (name, a.shape, b.shape)
    tm, tn, tk = _tile(m, tm), _tile(n, tn), _tile(k, tk)
    nk = k // tk
    dims = {"nn": NN, "tn": TN, "nt": NT}[mode]
    a_spec = {"nn": pl.BlockSpec((tm, tk), lambda i, j, kk: (i, kk)),
              "tn": pl.BlockSpec((tk, tm), lambda i, j, kk: (kk, i)),
              "nt": pl.BlockSpec((tm, tk), lambda i, j, kk: (i, kk))}[mode]
    b_spec = {"nn": pl.BlockSpec((tk, tn), lambda i, j, kk: (kk, j)),
              "tn": pl.BlockSpec((tk, tn), lambda i, j, kk: (kk, j)),
              "nt": pl.BlockSpec((tn, tk), lambda i, j, kk: (j, kk))}[mode]
    ins, in_specs = [a, b], [a_spec, b_spec]
    if bias is not None:
        ins.append(bias)
        in_specs.append(pl.BlockSpec((1, tn), lambda i, j, kk: (0, j)))
    if res is not None:
        ins.append(res)
        in_specs.append(pl.BlockSpec((tm, tn), lambda i, j, kk: (i, j)))
    has_bias, has_res = bias is not None, res is not None

    def body(*refs):
        a_ref, b_ref = refs[0], refs[1]
        pos = 2
        bias_ref = res_ref = None
        if has_bias:
            bias_ref = refs[pos]
            pos += 1
        if has_res:
            res_ref = refs[pos]
            pos += 1
        o_ref = refs[pos]
        part = _dotb(a_ref[...], b_ref[...], dims)

        def finish(r):
            if has_bias:
                r = r + bias_ref[...]
            if has_res:
                r = r + res_scale * res_ref[...]
            o_ref[...] = r.astype(out_dtype)

        if nk == 1:
            finish(part)
        else:
            acc_ref = refs[pos + 1]
            kk = pl.program_id(2)
            _accum(acc_ref, part, kk == 0)

            @pl.when(kk == nk - 1)
            def _():
                finish(acc_ref[...])

    return pl.pallas_call(
        body, name=name, grid=(m // tm, n // tn, nk), in_specs=in_specs,
        out_specs=pl.BlockSpec((tm, tn), lambda i, j, kk: (i, j)),
        out_shape=jax.ShapeDtypeStruct((m, n), out_dtype),
        scratch_shapes=[pltpu.VMEM((tm, tn), F32)] if nk > 1 else [], compiler_params=_params(3),
    )(*ins)


def _ew(name, fn, grid, ins, in_specs, out_shapes, out_specs, scratch=(), side=None):
    s_ins, s_outs, s_sems, s_start, s_wait = side if side is not None else ((), (), (), None, None)
    n_in, n_out, n_si, n_so, n_scr = len(ins), len(out_shapes), len(s_ins), len(s_outs), len(scratch)

    def body(*refs):
        pos = [0]

        def take(n):
            part = refs[pos[0]:pos[0] + n]
            pos[0] += n
            return part

        i_refs, si_refs, o_refs, so_refs, scr, sems = take(n_in), take(n_si), take(n_out), take(n_so), take(n_scr), take(len(s_sems))
        if side is not None:
            ids = [pl.program_id(a) for a in range(len(grid))]
            first = functools.reduce(jnp.logical_and, [p == 0 for p in ids])
            last = functools.reduce(jnp.logical_and, [p == g - 1 for p, g in zip(ids, grid)])

            @pl.when(first)
            def _():
                s_start(si_refs, so_refs, sems)

        fn(i_refs, o_refs, scr)
        if side is not None:
            @pl.when(last)
            def _():
                s_wait(si_refs, so_refs, sems)

    out = pl.pallas_call(
        body, name=name, grid=grid, in_specs=list(in_specs) + [ANY] * n_si, out_specs=list(out_specs) + [ANY] * n_so,
        out_shape=list(out_shapes) + list(s_outs), scratch_shapes=list(scratch) + list(s_sems),
        compiler_params=_params(len(grid)),
    )(*ins, *s_ins)
    return out


def _sds(shape, dtype=F32):
    return jax.ShapeDtypeStruct(tuple(shape), dtype)


def _rows(bs, w, col=0):
    return pl.BlockSpec((bs, w), lambda i, col=col: (i, col))


def _par(r, w):
    return pl.BlockSpec((r, w), lambda i: (0, 0))


def _ln_stats(r):
    mu = jnp.mean(r, axis=-1, keepdims=True)
    xc = r - mu
    var = jnp.mean(xc * xc, axis=-1, keepdims=True)
    rstd = lax.rsqrt(var + LN_EPS)
    return xc * rstd, rstd


def _ln_back(xh, rstd, g, dy):
    dxh = dy * g
    m1 = jnp.mean(dxh, axis=-1, keepdims=True)
    m2 = jnp.mean(dxh * xh, axis=-1, keepdims=True)
    return rstd * (dxh - m1 - xh * m2)


def _colsum(v):
    return jnp.sum(v, axis=0, keepdims=True)


def _ln_res_fwd(x, sub, g, b, alpha):
    s, d = x.shape
    bs = min(ROW_BLOCK, s)

    def fn(i_refs, o_refs, _):
        x_ref, s_ref, g_ref, b_ref = i_refs
        xh, _r = _ln_stats(alpha * x_ref[...] + s_ref[...])
        y = xh * g_ref[...] + b_ref[...]
        o_refs[0][...] = y
        o_refs[1][...] = y.astype(BF16)

    return _ew("ln_res_fwd", fn, (s // bs,), [x, sub, g, b], [_rows(bs, d), _rows(bs, d), _par(1, d), _par(1, d)],
               [_sds((s, d)), _sds((s, d), BF16)], [_rows(bs, d), _rows(bs, d)])


def _ln_res_bwd(x, sub, g, dy, alpha):
    s, d = x.shape
    bs = min(ROW_BLOCK, s)

    def fn(i_refs, o_refs, _):
        x_ref, s_ref, g_ref, dy_ref = i_refs
        dr_ref, drb_ref, dg_ref, db_ref = o_refs
        first = pl.program_id(0) == 0
        xh, rstd = _ln_stats(alpha * x_ref[...] + s_ref[...])
        dy_v = dy_ref[...]
        dr = _ln_back(xh, rstd, g_ref[...], dy_v)
        dr_ref[...] = dr
        drb_ref[...] = dr.astype(BF16)
        _accum(dg_ref, _colsum(dy_v * xh), first)
        _accum(db_ref, _colsum(dy_v), first)

    return _ew("ln_res_bwd", fn, (s // bs,), [x, sub, g, dy], [_rows(bs, d), _rows(bs, d), _par(1, d), _rows(bs, d)],
               [_sds((s, d)), _sds((s, d), BF16), _sds((1, d)), _sds((1, d))],
               [_rows(bs, d), _rows(bs, d), _par(1, d), _par(1, d)])


def _loss_fwd_bwd(y, t):
    s, d = y.shape
    bs = min(ROW_BLOCK, s)

    def fn(i_refs, o_refs, _):
        err = i_refs[0][...] - i_refs[1][...]
        o_refs[1][...] = err * (1.0 / d)
        tot = jnp.sum(jnp.sum(err * err, axis=1, keepdims=True), axis=0, keepdims=True) * (1.0 / d)
        _accum(o_refs[0], jnp.broadcast_to(tot, (8, LANE)), pl.program_id(0) == 0)

    return _ew("loss", fn, (s // bs,), [y, t], [_rows(bs, d), _rows(bs, d)],
               [_sds((8, LANE)), _sds((s, d))], [_par(8, LANE), _rows(bs, d)])


def _conv_fwd(z, cw, cb, g, b, d):
    s = z.shape[0]
    kw = cw.shape[0]
    bs = min(ROW_BLOCK, s)
    hb = CONV_HALO
    r = bs // hb

    def prev(col):
        return pl.BlockSpec((hb, d), lambda i: (jnp.maximum(i * r - 1, 0), col))

    def fn(i_refs, o_refs, scr):
        a_ref, b_ref, ap_ref, bp_ref, w_ref, cb_ref, g_ref, be_ref = i_refs
        c3_ref, c1_ref = o_refs
        ext = scr[0]
        i = pl.program_id(0)
        ext[pl.ds(0, hb), :] = jnp.where(i > 0, ap_ref[...] * _sig(bp_ref[...]), 0.0)
        ext[pl.ds(hb, bs), :] = a_ref[...] * _sig(b_ref[...])
        for st in range(d // LANE):
            sl = pl.ds(st * LANE, LANE)
            for rb in range(bs // CONV_ROWS):
                acc = jnp.zeros((CONV_ROWS, LANE), F32) + cb_ref[:, sl]
                for j in range(kw):
                    acc = acc + w_ref[pl.ds(j, 1), sl] * ext[pl.ds(rb * CONV_ROWS + hb - (kw - 1) + j, CONV_ROWS), sl]
                c1_ref[pl.ds(rb * CONV_ROWS, CONV_ROWS), sl] = acc
        xh, _r = _ln_stats(c1_ref[...])
        c2 = xh * g_ref[...] + be_ref[...]
        c3_ref[...] = (c2 * _sig(c2)).astype(BF16)

    return _ew("conv_fwd", fn, (s // bs,), [z, z, z, z, cw, cb, g, b],
               [_rows(bs, d, 0), _rows(bs, d, 1), prev(0), prev(1), _par(kw, d), _par(1, d), _par(1, d), _par(1, d)],
               [_sds((s, d), BF16), _sds((s, d))], [_rows(bs, d), _rows(bs, d)], scratch=[pltpu.VMEM((hb + bs, d), F32)])


def _conv_bwd(z, c1, dc3, cw, g, b, d, side=None):
    s = z.shape[0]
    kw = cw.shape[0]
    bs = min(ROW_BLOCK, s)
    hb = CONV_HALO
    r = bs // hb
    nrow = s // bs
    last_h = s // hb - 1

    def prev(col):
        return pl.BlockSpec((hb, d), lambda i: (jnp.maximum(i * r - 1, 0), col))

    def nxt(col):
        return pl.BlockSpec((hb, d), lambda i: (jnp.minimum((i + 1) * r, last_h), col))

    def fn(i_refs, o_refs, scr):
        a_ref, b_ref, ap_ref, bp_ref, c_ref, cn_ref, d_ref, dn_ref, w_ref, g_ref, be_ref = i_refs
        da_ref, db_ref, dw_ref, dcb_ref, dg_ref, dbe_ref = o_refs
        ext, extd = scr
        i = pl.program_id(0)
        first = i == 0
        more = i < nrow - 1
        ext[pl.ds(0, hb), :] = jnp.where(i > 0, ap_ref[...] * _sig(bp_ref[...]), 0.0)
        ext[pl.ds(hb, bs), :] = a_ref[...] * _sig(b_ref[...])
        c1_e = jnp.concatenate([c_ref[...], jnp.where(more, cn_ref[...], 0.0)], axis=0)
        xh, rstd = _ln_stats(c1_e)
        c2 = xh * g_ref[...] + be_ref[...]
        dc3_e = jnp.concatenate([d_ref[...], jnp.where(more, dn_ref[...], 0.0)], axis=0)
        dc2 = dc3_e * _dsilu(c2)
        dc1 = _ln_back(xh, rstd, g_ref[...], dc2)
        extd[...] = dc1
        _accum(dg_ref, _colsum((dc2 * xh)[:bs]), first)
        _accum(dbe_ref, _colsum(dc2[:bs]), first)
        _accum(dcb_ref, _colsum(dc1[:bs]), first)

        @pl.when(first)
        def _():
            dw_ref[...] = jnp.zeros_like(dw_ref)

        for st in range(d // LANE):
            sl = pl.ds(st * LANE, LANE)
            for rb in range(bs // CONV_ROWS):
                r0 = rb * CONV_ROWS
                own = extd[pl.ds(r0, CONV_ROWS), sl]
                dc0 = jnp.zeros((CONV_ROWS, LANE), F32)
                for j in range(kw):
                    dc0 = dc0 + w_ref[pl.ds(j, 1), sl] * extd[pl.ds(r0 + kw - 1 - j, CONV_ROWS), sl]
                    dw_ref[pl.ds(j, 1), sl] += _colsum(own * ext[pl.ds(r0 + hb - (kw - 1) + j, CONV_ROWS), sl])
                rows = pl.ds(r0, CONV_ROWS)
                a_v, sb = a_ref[rows, sl], _sig(b_ref[rows, sl])
                da_ref[rows, sl] = (dc0 * sb).astype(BF16)
                db_ref[rows, sl] = (dc0 * a_v * sb * (1.0 - sb)).astype(BF16)

    return _ew("conv_bwd", fn, (nrow,), [z, z, z, z, c1, c1, dc3, dc3, cw, g, b],
               [_rows(bs, d, 0), _rows(bs, d, 1), prev(0), prev(1), _rows(bs, d), nxt(0), _rows(bs, d), nxt(0),
                _par(kw, d), _par(1, d), _par(1, d)],
               [_sds((s, d), BF16), _sds((s, d), BF16), _sds((kw, d)), _sds((1, d)), _sds((1, d)), _sds((1, d))],
               [_rows(bs, d), _rows(bs, d), _par(kw, d), _par(1, d), _par(1, d), _par(1, d)],
               scratch=[pltpu.VMEM((bs + hb, d), F32), pltpu.VMEM((bs + hb, d), F32)], side=side)


def _sconv_fwd(z, sw, d, col0):
    s = z.shape[0]
    kw = sw.shape[0]
    bs = min(ROW_BLOCK, s)
    hb = SHORT_HALO
    r = bs // hb

    def fn(i_refs, o_refs, scr):
        x_ref, xp_ref, w_ref = i_refs
        ext = scr[0]
        i = pl.program_id(1)
        ext[pl.ds(0, hb), :] = jnp.where(i > 0, xp_ref[...], 0.0)
        ext[pl.ds(hb, bs), :] = x_ref[...]
        acc = jnp.zeros((bs, d), F32)
        for j in range(kw):
            acc = acc + w_ref[pl.ds(j, 1), :] * ext[pl.ds(hb - (kw - 1) + j, bs), :]
        o_refs[0][0] = acc * _sig(acc)

    return _ew("sconv_fwd", fn, (3, s // bs), [z, z, sw],
               [pl.BlockSpec((bs, d), lambda sg, i: (i, col0 + sg)),
                pl.BlockSpec((hb, d), lambda sg, i: (jnp.maximum(i * r - 1, 0), col0 + sg)),
                pl.BlockSpec((kw, d), lambda sg, i: (0, sg))],
               [_sds((3, s, d))], [pl.BlockSpec((1, bs, d), lambda sg, i: (sg, i, 0))],
               scratch=[pltpu.VMEM((hb + bs, d), F32)])[0]


def _sconv_bwd(z, dy, sw, d, col0):
    s = z.shape[0]
    kw = sw.shape[0]
    bs = min(ROW_BLOCK, s)
    hb = SHORT_HALO
    r = bs // hb
    nrow = s // bs
    last_h = s // hb - 1

    def fn(i_refs, o_refs, scr):
        x_ref, xp_ref, xn_ref, d_ref, dn_ref, w_ref = i_refs
        dx_ref, dw_ref = o_refs
        ext, extd = scr
        i = pl.program_id(1)
        more = i < nrow - 1
        ext[pl.ds(0, hb), :] = jnp.where(i > 0, xp_ref[...], 0.0)
        ext[pl.ds(hb, bs), :] = x_ref[...]
        ext[pl.ds(hb + bs, hb), :] = jnp.where(more, xn_ref[...], 0.0)
        n1 = bs + hb
        pre = jnp.zeros((n1, d), F32)
        for j in range(kw):
            pre = pre + w_ref[pl.ds(j, 1), :] * ext[pl.ds(hb - (kw - 1) + j, n1), :]
        dy_e = jnp.concatenate([d_ref[0], jnp.where(more, dn_ref[0], 0.0)], axis=0)
        extd[...] = dy_e * _dsilu(pre)

        @pl.when(i == 0)
        def _():
            dw_ref[...] = jnp.zeros_like(dw_ref)

        dp_own = extd[pl.ds(0, bs), :]
        dx = jnp.zeros((bs, d), F32)
        for j in range(kw):
            dx = dx + w_ref[pl.ds(j, 1), :] * extd[pl.ds(kw - 1 - j, bs), :]
            dw_ref[pl.ds(j, 1), :] += _colsum(dp_own * ext[pl.ds(hb - (kw - 1) + j, bs), :])
        dx_ref[0] = dx.astype(BF16)

    return _ew("sconv_bwd", fn, (3, nrow), [z, z, z, dy, dy, sw],
               [pl.BlockSpec((bs, d), lambda sg, i: (i, col0 + sg)),
                pl.BlockSpec((hb, d), lambda sg, i: (jnp.maximum(i * r - 1, 0), col0 + sg)),
                pl.BlockSpec((hb, d), lambda sg, i: (jnp.minimum((i + 1) * r, last_h), col0 + sg)),
                pl.BlockSpec((1, bs, d), lambda sg, i: (sg, i, 0)),
                pl.BlockSpec((1, hb, d), lambda sg, i: (sg, jnp.minimum((i + 1) * r, last_h), 0)),
                pl.BlockSpec((kw, d), lambda sg, i: (0, sg))],
               [_sds((3, s, d), BF16), _sds((kw, 3 * d))],
               [pl.BlockSpec((1, bs, d), lambda sg, i: (sg, i, 0)), pl.BlockSpec((kw, d), lambda sg, i: (0, sg))],
               scratch=[pltpu.VMEM((bs + 2 * hb, d), F32), pltpu.VMEM((bs + hb, d), F32)])


GDN_HEADS = 16

BNN = (((2,), (1,)), ((0,), (0,)))
BNT = (((2,), (2,)), ((0,), (0,)))
BTN = (((1,), (1,)), ((0,), (0,)))


def _bdot(a, b, dn):
    return lax.dot_general(a.astype(BF16), b.astype(BF16), dn, preferred_element_type=F32)


def _bdotf(a, b, dn):
    ah, al = _split(a)
    bh, bl = _split(b)

    def dot(u, v):
        return lax.dot_general(u, v, dn, preferred_element_type=F32)

    return dot(ah, bh) + (dot(ah, bl) + dot(al, bh))


def _gdn_pre(qkv_ref, ba_ref, bat_ref, hp_ref, hb):
    g0 = pl.program_id(0) * hb
    c = CHUNK

    def heads(part):
        return jnp.stack([qkv_ref[part, :, pl.ds(hh * HEAD, HEAD)] for hh in range(hb)], axis=0)

    qr, kr, v = heads(0), heads(1), heads(2)
    ba = ba_ref[...]
    bat = bat_ref[0]
    nh = bat.shape[0] // 2
    lane = _iota(ba.shape, 1)
    sub = _iota(bat.shape, 0)

    def col(off):
        return jnp.stack([jnp.sum(jnp.where(lane == g0 + hh + off, ba, 0.0), axis=1, keepdims=True) for hh in range(hb)], axis=0)

    def row(off):
        return jnp.stack([jnp.sum(jnp.where(sub == g0 + hh + off, bat, 0.0), axis=0, keepdims=True) for hh in range(hb)], axis=0)

    braw_c, araw_c, braw_r, araw_r = col(0), col(nh), row(0), row(nh)
    alog = jnp.max(hp_ref[:, pl.ds(0, 1), :], axis=2, keepdims=True)
    dtb = jnp.max(hp_ref[:, pl.ds(1, 1), :], axis=2, keepdims=True)
    nega = -jnp.exp(alog)
    beta_c, beta_r = _sig(braw_c), _sig(braw_r)
    la_c = nega * _softplus(araw_c + dtb)
    la_r = nega * _softplus(araw_r + dtb)
    i = _iota((c, c), 0)
    j = _iota((c, c), 1)
    g_c = jnp.sum(jnp.where(j <= i, la_r, 0.0), axis=2, keepdims=True)
    g_r = jnp.sum(jnp.where(i <= j, la_c, 0.0), axis=1, keepdims=True)
    g_last = jnp.sum(la_c, axis=1, keepdims=True)
    low = i >= j
    dec = jnp.where(low, jnp.exp(jnp.where(low, g_c - g_r, 0.0)), 0.0)
    rq = lax.rsqrt(jnp.sum(qr * qr, axis=2, keepdims=True) + L2_EPS)
    rk = lax.rsqrt(jnp.sum(kr * kr, axis=2, keepdims=True) + L2_EPS)
    q = qr * (rq * HEAD ** -0.5)
    k = kr * rk
    kb = k * beta_c
    lmat = jnp.where(i > j, _bdot(kb, k, BNT) * dec, 0.0)
    eye = jnp.where(i == j, 1.0, 0.0)
    tinv = eye - lmat
    pw = lmat
    for _ in range(int(np.log2(c)) - 1):
        pw = _bdotf(pw, pw, BNN)
        tinv = _bdotf(tinv, eye + pw, BNN)
    eg_c = jnp.exp(g_c)
    rhs_w = kb * eg_c
    sol = _bdotf(tinv, jnp.concatenate([v * beta_c, rhs_w], axis=2), BNN)
    attn = jnp.where(low, _bdot(q, k, BNT) * dec, 0.0)
    ekd = jnp.exp(g_last - g_c)
    return dict(qr=qr, kr=kr, v=v, rq=rq, rk=rk, q=q, k=k, kb=kb, beta_c=beta_c, beta_r=beta_r, la_r=la_r,
                araw_r=araw_r, dtb=dtb, nega=nega, g_c=g_c, g_last=g_last, dec=dec, lmat=lmat, tinv=tinv,
                eg_c=eg_c, rhs_w=rhs_w, sol=sol, u=sol[:, :, :HEAD], w=sol[:, :, HEAD:], attn=attn, q_dec=q * eg_c,
                ekd=ekd, k_dec=k * ekd, i=i, j=j, low=low)


def _gdn_fwd(qkv, ba, bat, hp, side=None):
    _, s, d = qkv.shape
    nh, n, c = d // HEAD, s // CHUNK, CHUNK
    hb = min(GDN_HEADS, nh)
    gw = hb * HEAD

    def fn(i_refs, o_refs, scr):
        o_ref, st_ref = o_refs
        s_scr = scr[0]
        st = jnp.where(pl.program_id(1) == 0, 0.0, s_scr[...])
        p = _gdn_pre(*i_refs, hb)
        vn = p["u"] - _bdot(p["w"], st, BNN)
        o = _bdot(p["q_dec"], st, BNN) + _bdot(p["attn"], vn, BNN)
        s_scr[...] = st * jnp.exp(p["g_last"]) + _bdot(p["k_dec"], vn, BTN)
        st_ref[:, 0] = st
        for hh in range(hb):
            o_ref[:, pl.ds(hh * HEAD, HEAD)] = o[hh]

    return _ew("gdn_fwd", fn, (nh // hb, n), [qkv, ba, bat, hp],
               [pl.BlockSpec((3, c, gw), lambda h, t: (0, t, h)), pl.BlockSpec((c, LANE), lambda h, t: (t, 0)),
                pl.BlockSpec((1, bat.shape[1], c), lambda h, t: (t, 0, 0)), pl.BlockSpec((hb, 8, LANE), lambda h, t: (h, 0, 0))],
               [_sds((s, d)), _sds((nh, n, HEAD, HEAD))],
               [pl.BlockSpec((c, gw), lambda h, t: (t, h)), pl.BlockSpec((hb, 1, HEAD, HEAD), lambda h, t: (h, t, 0, 0))],
               scratch=[pltpu.VMEM((hb, HEAD, HEAD), F32)], side=side)


def _gdn_bwd(qkv, ba, bat, hp, states, do):
    _, s, d = qkv.shape
    nh, n, c = d // HEAD, s // CHUNK, CHUNK
    hb = min(GDN_HEADS, nh)
    gw = hb * HEAD

    def fn(i_refs, o_refs, scr):
        qkv_ref, ba_ref, bat_ref, hp_ref, st_ref, do_ref = i_refs
        dqkv_ref, dbat_ref, dhp_ref = o_refs
        ds_scr = scr[0]
        first = pl.program_id(1) == 0
        ds1 = jnp.where(first, 0.0, ds_scr[...])
        dhp_old = jnp.where(first, 0.0, dhp_ref[...])
        st = st_ref[:, 0]
        do_v = jnp.stack([do_ref[:, pl.ds(hh * HEAD, HEAD)] for hh in range(hb)], axis=0)
        p = _gdn_pre(qkv_ref, ba_ref, bat_ref, hp_ref, hb)
        i, j, low = p["i"], p["j"], p["low"]
        u, w, sol, attn, dec = p["u"], p["w"], p["sol"], p["attn"], p["dec"]
        q, k, kb, v = p["q"], p["k"], p["kb"], p["v"]
        eg_c, ekd, beta_c = p["eg_c"], p["ekd"], p["beta_c"]
        egl = jnp.exp(p["g_last"])

        def rsum(a):
            return jnp.sum(a, axis=2, keepdims=True)

        def csum(a):
            return jnp.sum(a, axis=1, keepdims=True)

        vn = u - _bdot(w, st, BNN)
        dvn = _bdot(attn, do_v, BTN) + _bdot(p["k_dec"], ds1, BNN)
        dattn = jnp.where(low, _bdot(do_v, vn, BNT), 0.0)
        dqd = _bdot(do_v, st, BNT)
        dkd = _bdot(vn, ds1, BNT)
        ds_scr[...] = _bdot(p["q_dec"], do_v, BTN) + egl * ds1 - _bdot(w, dvn, BTN)
        dgl = csum(rsum(st * ds1)) * egl
        dw = -_bdot(dvn, st, BNT)
        drhs = _bdotf(p["tinv"], jnp.concatenate([dvn, dw], axis=2), BTN)
        da = -jnp.where(i > j, _bdot(drhs, sol, BNT), 0.0)
        mm = da * p["lmat"] + dattn * attn
        dg_c = rsum(mm)
        dg_r = -csum(mm)
        dkk = da * dec
        dqk = dattn * dec
        dkb = _bdot(dkk, k, BNN)
        dk = _bdot(dkk, kb, BTN) + _bdot(dqk, q, BTN)
        dq = _bdot(dqk, k, BNN) + dqd * eg_c
        dg_c = dg_c + rsum(dqd * p["q_dec"])
        dk = dk + dkd * ekd
        t = rsum(dkd * p["k_dec"])
        dgl = dgl + csum(t)
        dg_c = dg_c - t
        drhs_u, drhs_w = drhs[:, :, :HEAD], drhs[:, :, HEAD:]
        dkb = dkb + drhs_w * eg_c
        dg_c = dg_c + rsum(drhs_w * p["rhs_w"])
        dv_out = drhs_u * beta_c
        dbeta_c = rsum(drhs_u * v) + rsum(dkb * k)
        dk = dk + dkb * beta_c
        diag = i == j
        dg = dg_c + rsum(jnp.where(diag, dg_r, 0.0))
        dla_r = csum(jnp.where(low, dg, 0.0)) + dgl
        dbeta_r = csum(jnp.where(diag, dbeta_c, 0.0))
        beta_r = p["beta_r"]
        dbraw_r = dbeta_r * beta_r * (1.0 - beta_r)
        daraw_r = dla_r * p["nega"] * _sig(p["araw_r"] + p["dtb"])
        dalog = rsum(dla_r * p["la_r"])
        ddtb = rsum(daraw_r)
        row8 = _iota((8, c), 0)
        dbat_ref[:, 0] = jnp.where(row8 == 0, dbraw_r, jnp.where(row8 == 1, daraw_r, 0.0))
        rowp = _iota((8, LANE), 0)
        dhp_ref[...] = dhp_old + jnp.where(rowp == 0, dalog, jnp.where(rowp == 1, ddtb, 0.0))
        qr, kr, rq, rk = p["qr"], p["kr"], p["rq"], p["rk"]
        sc = HEAD ** -0.5
        dq_out = sc * (rq * dq - qr * (rq * rq * rq * rsum(dq * qr)))
        dk_out = rk * dk - kr * (rk * rk * rk * rsum(dk * kr))
        for hh in range(hb):
            sl = pl.ds(hh * HEAD, HEAD)
            dqkv_ref[0, :, sl] = dq_out[hh]
            dqkv_ref[1, :, sl] = dk_out[hh]
            dqkv_ref[2, :, sl] = dv_out[hh]

    rev = n - 1
    return _ew("gdn_bwd", fn, (nh // hb, n), [qkv, ba, bat, hp, states, do],
               [pl.BlockSpec((3, c, gw), lambda h, t: (0, rev - t, h)), pl.BlockSpec((c, LANE), lambda h, t: (rev - t, 0)),
                pl.BlockSpec((1, bat.shape[1], c), lambda h, t: (rev - t, 0, 0)), pl.BlockSpec((hb, 8, LANE), lambda h, t: (h, 0, 0)),
                pl.BlockSpec((hb, 1, HEAD, HEAD), lambda h, t: (h, rev - t, 0, 0)), pl.BlockSpec((c, gw), lambda h, t: (rev - t, h))],
               [_sds((3, s, d)), _sds((nh, n, 8, c)), _sds((nh, 8, LANE))],
               [pl.BlockSpec((3, c, gw), lambda h, t: (0, rev - t, h)), pl.BlockSpec((hb, 1, 8, c), lambda h, t: (h, rev - t, 0, 0)),
                pl.BlockSpec((hb, 8, LANE), lambda h, t: (h, 0, 0))],
               scratch=[pltpu.VMEM((hb, HEAD, HEAD), F32)])


def _gdn_post_fwd(o, z, gw, d, zcol):
    s = o.shape[0]
    bs = min(ROW_BLOCK, s)

    def fn(i_refs, o_refs, _):
        o_ref, z_ref, w_ref = i_refs
        wv = w_ref[...]
        for h in range(d // HEAD):
            sl = pl.ds(h * HEAD, HEAD)
            oh, zg = o_ref[:, sl], z_ref[:, sl]
            r = lax.rsqrt(jnp.mean(oh * oh, axis=1, keepdims=True) + L2_EPS)
            o_refs[0][:, sl] = (oh * r * wv * (zg * _sig(zg))).astype(BF16)

    return _ew("gdn_post_fwd", fn, (s // bs,), [o, z, gw], [_rows(bs, d), _rows(bs, d, zcol), _par(1, HEAD)],
               [_sds((s, d), BF16)], [_rows(bs, d)])[0]


def _gdn_post_bwd(dog, o, z, gw, d, zcol):
    s = o.shape[0]
    bs = min(ROW_BLOCK, s)

    def fn(i_refs, o_refs, _):
        g_ref, o_ref, z_ref, w_ref = i_refs
        do_ref, dz_ref, dw_ref = o_refs
        wv = w_ref[...]
        dwacc = jnp.zeros((1, HEAD), F32)
        for h in range(d // HEAD):
            sl = pl.ds(h * HEAD, HEAD)
            oh, zg, gv = o_ref[:, sl], z_ref[:, sl], g_ref[:, sl]
            r = lax.rsqrt(jnp.mean(oh * oh, axis=1, keepdims=True) + L2_EPS)
            on = oh * r
            sil = zg * _sig(zg)
            dz_ref[:, sl] = (gv * on * wv * _dsilu(zg)).astype(BF16)
            dwacc = dwacc + _colsum(gv * on * sil)
            don = gv * wv * sil
            do_ref[:, sl] = r * (don - on * jnp.mean(don * on, axis=1, keepdims=True))
        _accum(dw_ref, dwacc, pl.program_id(0) == 0)

    return _ew("gdn_post_bwd", fn, (s // bs,), [dog, o, z, gw],
               [_rows(bs, d), _rows(bs, d), _rows(bs, d, zcol), _par(1, HEAD)],
               [_sds((s, d)), _sds((s, d), BF16), _sds((1, HEAD))], [_rows(bs, d), _rows(bs, d), _par(1, HEAD)])


def _merge_fwd(z, yc, yg, d, col_a, col_b):
    s = z.shape[0]
    bs = min(ROW_BLOCK, s)

    def fn(i_refs, o_refs, _):
        ga, gb, yc_ref, yg_ref = i_refs
        o_refs[0][...] = (_sig(ga[...]) * yc_ref[...] + _sig(gb[...]) * yg_ref[...]).astype(BF16)

    return _ew("merge_fwd", fn, (s // bs,), [z, z, yc, yg],
               [_rows(bs, d, col_a), _rows(bs, d, col_b), _rows(bs, d), _rows(bs, d)], [_sds((s, d), BF16)], [_rows(bs, d)])[0]


def _merge_bwd(dm, z, yc, yg, d, col_a, col_b):
    s = z.shape[0]
    bs = min(ROW_BLOCK, s)

    def fn(i_refs, o_refs, _):
        dm_ref, ga, gb, yc_ref, yg_ref = i_refs
        dyc_ref, dyg_ref, dga_ref, dgb_ref, dbc_ref = o_refs
        dmv = dm_ref[...]
        sa, sb = _sig(ga[...]), _sig(gb[...])
        dyc = dmv * sa
        dyc_ref[...] = dyc.astype(BF16)
        dyg_ref[...] = (dmv * sb).astype(BF16)
        dga_ref[...] = (dmv * yc_ref[...] * sa * (1.0 - sa)).astype(BF16)
        dgb_ref[...] = (dmv * yg_ref[...] * sb * (1.0 - sb)).astype(BF16)
        _accum(dbc_ref, _colsum(dyc), pl.program_id(0) == 0)

    return _ew("merge_bwd", fn, (s // bs,), [dm, z, z, yc, yg],
               [_rows(bs, d), _rows(bs, d, col_a), _rows(bs, d, col_b), _rows(bs, d), _rows(bs, d)],
               [_sds((s, d), BF16)] * 4 + [_sds((1, d))], [_rows(bs, d)] * 4 + [_par(1, d)])


def _swiglu_fwd(hg, hu):
    s, f = hg.shape
    bs = min(ROW_BLOCK, s)
    cw = _tile(f, 2048)

    def fn(i_refs, o_refs, _):
        g = i_refs[0][...]
        o_refs[0][...] = (g * _sig(g) * i_refs[1][...]).astype(BF16)

    spec = pl.BlockSpec((bs, cw), lambda i, j: (i, j))
    return _ew("swiglu_fwd", fn, (s // bs, f // cw), [hg, hu], [spec, spec], [_sds((s, f), BF16)], [spec])[0]


def _swiglu_bwd(hg, hu, df):
    s, f = hg.shape
    bs = min(ROW_BLOCK, s)
    cw = _tile(f, 2048)

    def fn(i_refs, o_refs, _):
        g, u, dfv = i_refs[0][...], i_refs[1][...], i_refs[2][...]
        o_refs[0][...] = (dfv * u * _dsilu(g)).astype(BF16)
        o_refs[1][...] = (dfv * g * _sig(g)).astype(BF16)

    spec = pl.BlockSpec((bs, cw), lambda i, j: (i, j))
    return _ew("swiglu_bwd", fn, (s // bs, f // cw), [hg, hu, df], [spec] * 3, [_sds((s, f), BF16)] * 2, [spec] * 2)


def _col_sums(name, a):
    s, n = a.shape
    bs = min(ROW_BLOCK, s)
    cw = _tile(n, 2048)

    def fn(i_refs, o_refs, _):
        _accum(o_refs[0], _colsum(i_refs[0][...].astype(F32)), pl.program_id(1) == 0)

    return _ew(name, fn, (n // cw, s // bs), [a], [pl.BlockSpec((bs, cw), lambda j, i: (i, j))],
               [_sds((1, n))], [pl.BlockSpec((1, cw), lambda j, i: (0, j))])[0]


def _adamw(name, w, g, m, v):
    r, c = w.shape
    br = r
    if r * c * 4 > (1 << 20):
        cands = [t for t in range(8, r, 8) if r % t == 0 and t * c * 4 <= (1 << 20)]
        br = max(cands) if cands else r
    c1 = 1.0 - ADAM_B1 ** ADAM_STEP
    c2 = 1.0 - ADAM_B2 ** ADAM_STEP

    def fn(i_refs, o_refs, _):
        wv, gv, mv, vv = (x[...] for x in i_refs)
        m2 = ADAM_B1 * mv + (1.0 - ADAM_B1) * gv
        v2 = ADAM_B2 * vv + (1.0 - ADAM_B2) * (gv * gv)
        o_refs[0][...] = -ADAM_LR * ((m2 / c1) / (jnp.sqrt(v2 / c2) + ADAM_EPS) + ADAM_WD * wv)
        o_refs[1][...] = m2
        o_refs[2][...] = v2

    spec = pl.BlockSpec((br, c), lambda i: (i, 0))
    return _ew(name, fn, (r // br,), [w, g, m, v], [spec] * 4, [_sds((r, c))] * 3, [spec] * 3)


ANY = pl.BlockSpec(memory_space=pl.ANY)


def _place():
    return lax.axis_index("x"), lax.axis_index("y"), lax.axis_index("c")


def _gather_chips(flat):
    r, w = flat.shape

    def body(x_ref, o_ref, ssem, rsem, lsem):
        x, y, c = _place()
        me = 2 * x + y
        chips = [(1 - x, y), (x, 1 - y), (1 - x, 1 - y)]
        local = pltpu.make_async_copy(x_ref, o_ref.at[me], lsem)
        local.start()
        sends = []
        for j, (px, py) in enumerate(chips):
            cp = pltpu.make_async_remote_copy(src_ref=x_ref, dst_ref=o_ref.at[me], send_sem=ssem.at[j], recv_sem=rsem.at[j],
                                              device_id=(px, py, c), device_id_type=MESH)
            cp.start()
            sends.append(cp)
        for j, (px, py) in enumerate(chips):
            pltpu.make_async_remote_copy(src_ref=x_ref, dst_ref=o_ref.at[2 * px + py], send_sem=ssem.at[j], recv_sem=rsem.at[j],
                                         device_id=(px, py, c), device_id_type=MESH).wait_recv()
        for cp in sends:
            cp.wait_send()
        local.wait()

    return pl.pallas_call(
        body, name="gather_chips", in_specs=[ANY], out_specs=ANY, out_shape=_sds((4, r, w), flat.dtype),
        scratch_shapes=[pltpu.SemaphoreType.DMA((3,)), pltpu.SemaphoreType.DMA((3,)), pltpu.SemaphoreType.DMA(())],
    )(flat)


def _gather_all(flat):
    r, w = flat.shape
    masks = [(mx, my, mc) for mx in (0, 1) for my in (0, 1) for mc in (0, 1)][1:]

    def body(x_ref, o_ref, ssem, rsem, lsem):
        x, y, c = _place()
        me = 4 * x + 2 * y + c
        local = pltpu.make_async_copy(x_ref, o_ref.at[me], lsem)
        local.start()
        peers = [(x ^ mx, y ^ my, c ^ mc) for (mx, my, mc) in masks]
        sends = []
        for j, peer in enumerate(peers):
            cp = pltpu.make_async_remote_copy(src_ref=x_ref, dst_ref=o_ref.at[me], send_sem=ssem.at[j], recv_sem=rsem.at[j],
                                              device_id=peer, device_id_type=MESH)
            cp.start()
            sends.append(cp)
        for j, (px, py, pc) in enumerate(peers):
            pltpu.make_async_remote_copy(src_ref=x_ref, dst_ref=o_ref.at[4 * px + 2 * py + pc], send_sem=ssem.at[j],
                                         recv_sem=rsem.at[j], device_id=(px, py, pc), device_id_type=MESH).wait_recv()
        for cp in sends:
            cp.wait_send()
        local.wait()

    return pl.pallas_call(
        body, name="gather_all", in_specs=[ANY], out_specs=ANY, out_shape=_sds((8, r, w), flat.dtype),
        scratch_shapes=[pltpu.SemaphoreType.DMA((7,)), pltpu.SemaphoreType.DMA((7,)), pltpu.SemaphoreType.DMA(())],
    )(flat)


def _sum8(parts):
    _, r, w = parts.shape

    def fn(i_refs, o_refs, _):
        acc = i_refs[0][0]
        for k in range(1, 8):
            acc = acc + i_refs[0][k]
        o_refs[0][...] = acc

    br = _tile(r, 512, 8)
    return _ew("sum8", fn, (r // br,), [parts], [pl.BlockSpec((8, br, w), lambda i: (0, i, 0))], [_sds((r, w))],
               [pl.BlockSpec((br, w), lambda i: (i, 0))])[0]


def _swap_halves(g):
    _, r, w = g.shape
    rh = r // 2

    def body(g_ref, o_ref, ssem, rsem):
        x, y, c = _place()
        cp = pltpu.make_async_remote_copy(src_ref=g_ref.at[:, pl.ds((1 - c) * rh, rh), :], dst_ref=o_ref, send_sem=ssem,
                                          recv_sem=rsem, device_id=(x, y, 1 - c), device_id_type=MESH)
        cp.start()
        cp.wait()

    return pl.pallas_call(
        body, name="swap_halves", in_specs=[ANY], out_specs=ANY, out_shape=_sds((4, rh, w), g.dtype),
        scratch_shapes=[pltpu.SemaphoreType.DMA(()), pltpu.SemaphoreType.DMA(())],
    )(g)


def _add_half(g, got, c_arr):
    _, r, w = g.shape
    rh = r // 2
    br = _tile(rh, 512, 16)
    nb = rh // br

    def body(c_ref, g_ref, r_ref, o_ref):
        o_ref[...] = (g_ref[...] + r_ref[...]).astype(BF16)

    return pl.pallas_call(
        body, name="add_half",
        grid_spec=pltpu.PrefetchScalarGridSpec(
            num_scalar_prefetch=1, grid=(4, nb),
            in_specs=[pl.BlockSpec((1, br, w), lambda s, i, c_ref: (s, c_ref[0] * nb + i, 0)),
                      pl.BlockSpec((1, br, w), lambda s, i, c_ref: (s, i, 0))],
            out_specs=pl.BlockSpec((1, br, w), lambda s, i, c_ref: (s, i, 0))),
        out_shape=_sds((4, rh, w), BF16), compiler_params=_params(2),
    )(c_arr, g, got)


def _scatter_side(p):
    _, rh, w = p.shape

    def copies(p_ref, o_ref, ssem, rsem):
        x, y, c = _place()
        chips = [(1 - x, y), (x, 1 - y), (1 - x, 1 - y)]
        return [pltpu.make_async_remote_copy(src_ref=p_ref.at[2 * px + py], dst_ref=o_ref.at[j], send_sem=ssem.at[j],
                                             recv_sem=rsem.at[j], device_id=(px, py, c), device_id_type=MESH)
                for j, (px, py) in enumerate(chips)]

    def start(si, so, sems):
        for cp in copies(si[0], so[0], *sems):
            cp.start()

    def wait(si, so, sems):
        cps = copies(si[0], so[0], *sems)
        for cp in cps:
            cp.wait_recv()
        for cp in cps:
            cp.wait_send()

    return [p], [_sds((3, rh, w), p.dtype)], [pltpu.SemaphoreType.DMA((3,)), pltpu.SemaphoreType.DMA((3,))], start, wait


def _run_side(name, side):
    s_ins, s_outs, s_sems, s_start, s_wait = side
    n_in, n_out = len(s_ins), len(s_outs)

    def body(*refs):
        si, so, sems = refs[:n_in], refs[n_in:n_in + n_out], refs[n_in + n_out:]
        s_start(si, so, sems)
        s_wait(si, so, sems)

    return pl.pallas_call(body, name=name, in_specs=[ANY] * n_in, out_specs=[ANY] * n_out, out_shape=list(s_outs),
                          scratch_shapes=list(s_sems))(*s_ins)


def _add_chips(p, got, idx_arr):
    _, rh, w = p.shape
    br = _tile(rh, 512, 16)
    nb = rh // br

    def body(idx_ref, p_ref, r_ref, o_ref):
        o_ref[...] = ((p_ref[0].astype(F32) + r_ref[0].astype(F32)) + r_ref[1].astype(F32)) + r_ref[2].astype(F32)

    return pl.pallas_call(
        body, name="add_chips",
        grid_spec=pltpu.PrefetchScalarGridSpec(
            num_scalar_prefetch=1, grid=(nb,),
            in_specs=[pl.BlockSpec((1, br, w), lambda i, idx: (idx[0], i, 0)),
                      pl.BlockSpec((3, br, w), lambda i, idx: (0, i, 0))],
            out_specs=pl.BlockSpec((br, w), lambda i, idx: (idx[1] * nb + i, 0))),
        out_shape=_sds((2 * rh, w)), compiler_params=_params(1),
    )(idx_arr, p, got)


def _join_halves(buf):
    r, w = buf.shape
    rh = r // 2

    def body(b_ref, o_ref, ssem, rsem):
        x, y, c = _place()
        mine = o_ref.at[pl.ds(c * rh, rh), :]
        cp = pltpu.make_async_remote_copy(src_ref=mine, dst_ref=mine, send_sem=ssem, recv_sem=rsem,
                                          device_id=(x, y, 1 - c), device_id_type=MESH)
        cp.start()
        other = o_ref.at[pl.ds((1 - c) * rh, rh), :]
        pltpu.make_async_remote_copy(src_ref=other, dst_ref=other, send_sem=ssem, recv_sem=rsem,
                                     device_id=(x, y, 1 - c), device_id_type=MESH).wait_recv()
        cp.wait_send()

    return pl.pallas_call(
        body, name="join_halves", in_specs=[ANY], out_specs=ANY, out_shape=_sds((r, w), buf.dtype),
        input_output_aliases={0: 0},
        scratch_shapes=[pltpu.SemaphoreType.DMA(()), pltpu.SemaphoreType.DMA(())],
    )(buf)


def _gather_side(flat):
    r, w = flat.shape
    rh = r // 2

    def ici(x_ref, o_ref, ssem, rsem):
        x, y, c = _place()
        chips = [(1 - x, y), (x, 1 - y), (1 - x, 1 - y)]
        mine = pl.ds(c * rh, rh)
        return [pltpu.make_async_remote_copy(src_ref=x_ref.at[mine, :], dst_ref=o_ref.at[2 * x + y, mine, :], send_sem=ssem.at[j],
                                             recv_sem=rsem.at[j], device_id=(px, py, c), device_id_type=MESH)
                for j, (px, py) in enumerate(chips)]

    def start(si, so, sems):
        for cp in ici(si[0], so[0], *sems):
            cp.start()

    def wait(si, so, sems):
        x_ref, o_ref = si[0], so[0]
        ssem, rsem = sems
        x, y, c = _place()
        chips = [(1 - x, y), (x, 1 - y), (1 - x, 1 - y)]
        sib = (x, y, 1 - c)
        mine = pl.ds(c * rh, rh)
        other = pl.ds((1 - c) * rh, rh)
        sends = ici(x_ref, o_ref, ssem, rsem)
        for j, (px, py) in enumerate(chips):
            got = o_ref.at[2 * px + py, mine, :]
            pltpu.make_async_remote_copy(src_ref=x_ref.at[mine, :], dst_ref=got, send_sem=ssem.at[j], recv_sem=rsem.at[j],
                                         device_id=(px, py, c), device_id_type=MESH).wait_recv()
            cp = pltpu.make_async_remote_copy(src_ref=got, dst_ref=got, send_sem=ssem.at[3 + j], recv_sem=rsem.at[3 + j],
                                              device_id=sib, device_id_type=MESH)
            cp.start()
            sends.append(cp)
        for j, (px, py) in enumerate(chips):
            theirs = o_ref.at[2 * px + py, other, :]
            pltpu.make_async_remote_copy(src_ref=theirs, dst_ref=theirs, send_sem=ssem.at[3 + j], recv_sem=rsem.at[3 + j],
                                         device_id=sib, device_id_type=MESH).wait_recv()
        for cp in sends:
            cp.wait_send()

    return ([flat], [_sds((4, r, w), flat.dtype)], [pltpu.SemaphoreType.DMA((6,)), pltpu.SemaphoreType.DMA((6,))],
            start, wait)


BIG = ("w_in", "w_conv_proj", "w_gdn_proj", "w_out", "w_ffn_in", "w_ffn_out")
COL_SHARDED = ("w_in", "w_ffn_in")
SMALL = ("b_in", "conv_dw_w", "conv_dw_b", "conv_ln_g", "conv_ln_b", "b_conv_proj", "short_conv_w", "a_log", "dt_bias",
         "gdn_norm_w", "ln1_g", "ln1_b", "ln2_g", "ln2_b")
SMALL_SHARDED = ("conv_dw_w", "short_conv_w")
ORDER = ("w_in", "b_in", "conv_dw_w", "conv_dw_b", "conv_ln_g", "conv_ln_b", "w_conv_proj", "b_conv_proj", "short_conv_w",
         "a_log", "dt_bias", "gdn_norm_w", "w_gdn_proj", "w_out", "ln1_g", "ln1_b", "w_ffn_in", "w_ffn_out", "ln2_g", "ln2_b")


def _flat_rows(shapes, fw):
    rows = [int(np.prod(sh)) // fw for sh in shapes]
    total = sum(rows)
    pad = (-total) % (2048 if total >= 4096 else 32)
    return rows, total + pad, pad


def _row(v):
    return v.reshape(1, -1)


def _layer_fwd(x, xb, wl, alpha, side=None):
    d = x.shape[1]
    z = _mm("mm_z", xb, wl["w_main"], "nn", bias=wl["b_main"])
    ba = _mm("mm_ba", xb, wl["w_ba"], "nn", bias=wl["b_ba"])
    c3, c1 = _conv_fwd(z, wl["conv_dw_w"], wl["conv_dw_b"], wl["conv_ln_g"], wl["conv_ln_b"], d)
    yc = _mm("mm_proj", c3, wl["w_conv_proj"], "nn", bias=wl["b_conv_proj"])
    qkv = _sconv_fwd(z, wl["short_conv_w"], d, 2)
    s = x.shape[0]
    nh = d // HEAD
    bat = jnp.transpose(ba[:, :2 * nh].reshape(s // CHUNK, CHUNK, 2 * nh), (0, 2, 1))
    o, states, *side_out = _gdn_fwd(qkv, ba, bat, wl["hp"], side)
    og = _gdn_post_fwd(o, z, wl["gdn_norm_w"], d, 5)
    yg = _mm("mm_proj", og, wl["w_gdn_proj"], "nn", bias=wl["zero_bias"])
    m = _merge_fwd(z, yc, yg, d, 6, 7)
    mix = _mm("mm_proj", m, wl["w_out"], "nn", bias=wl["zero_bias"])
    x1, x1b = _ln_res_fwd(x, mix, wl["ln1_g"], wl["ln1_b"], alpha)
    hg = _mm("mm_ffn_in", x1b, wl["w_ffn_g"], "nn")
    hu = _mm("mm_ffn_in", x1b, wl["w_ffn_u"], "nn")
    f = _swiglu_fwd(hg, hu)
    ffn = _mm("mm_ffn_out", f, wl["w_ffn_out"], "nn")
    x2, x2b = _ln_res_fwd(x1, ffn, wl["ln2_g"], wl["ln2_b"], alpha)
    saved = dict(x=x, xb=xb, x1b=x1b, z=z, ba=ba, bat=bat, c1=c1, c3=c3, yc=yc, qkv=qkv, o=o, states=states, og=og, yg=yg, m=m, mix=mix,
                 x1=x1, hg=hg, hu=hu, f=f, ffn=ffn)
    return x2, x2b, saved, side_out


def _layer_bwd(dy, sv, wl, alpha, side=None):
    x, z, x1, xb, x1b = sv["x"], sv["z"], sv["x1"], sv["xb"], sv["x1b"]
    s, d = x.shape
    nh = d // HEAD
    g = {}
    dr2, dr2b, g["ln2_g"], g["ln2_b"] = _ln_res_bwd(x1, sv["ffn"], wl["ln2_g"], dy, alpha)
    df = _mm("mm_dffn_out", dr2b, wl["w_ffn_out"], "nt")
    g["w_ffn_out"] = _mm("mm_gw_ffn_out", sv["f"], dr2b, "tn")
    dhg, dhu = _swiglu_bwd(sv["hg"], sv["hu"], df)
    t = _mm("mm_dffn_in", dhg, wl["w_ffn_g"], "nt", res=dr2, res_scale=alpha)
    dx1 = _mm("mm_dffn_in", dhu, wl["w_ffn_u"], "nt", res=t)
    g["w_ffn_in"] = jnp.concatenate([_mm("mm_gw_ffn_in", x1b, dhg, "tn"), _mm("mm_gw_ffn_in", x1b, dhu, "tn")], axis=1)
    dr1, dr1b, g["ln1_g"], g["ln1_b"] = _ln_res_bwd(x, sv["mix"], wl["ln1_g"], dx1, alpha)
    dm = _mm("mm_dproj", dr1b, wl["w_out"], "nt")
    g["w_out"] = _mm("mm_gw_proj", sv["m"], dr1b, "tn")
    dyc, dyg, dga, dgb, g["b_conv_proj"] = _merge_bwd(dm, z, sv["yc"], sv["yg"], d, 6, 7)
    dc3 = _mm("mm_dproj", dyc, wl["w_conv_proj"], "nt")
    g["w_conv_proj"] = _mm("mm_gw_proj", sv["c3"], dyc, "tn")
    dog = _mm("mm_dproj", dyg, wl["w_gdn_proj"], "nt")
    g["w_gdn_proj"] = _mm("mm_gw_proj", sv["og"], dyg, "tn")
    dglu_a, dglu_b, g["conv_dw_w"], g["conv_dw_b"], g["conv_ln_g"], g["conv_ln_b"], *side_out = _conv_bwd(
        z, sv["c1"], dc3, wl["conv_dw_w"], wl["conv_ln_g"], wl["conv_ln_b"], d, side)
    do, dzg, g["gdn_norm_w"] = _gdn_post_bwd(dog, sv["o"], z, wl["gdn_norm_w"], d, 5)
    dqkv, dbat, dhp = _gdn_bwd(sv["qkv"], sv["ba"], sv["bat"], wl["hp"], sv["states"], do)
    g["a_log"] = dhp[:, 0, 0]
    g["dt_bias"] = dhp[:, 1, 0]
    dzq, g["short_conv_w"] = _sconv_bwd(z, dqkv, wl["short_conv_w"], d, 2)
    dz = jnp.concatenate([dglu_a, dglu_b, dzq[0], dzq[1], dzq[2], dzg, dga, dgb], axis=1)
    dba = jnp.transpose(dbat[:, :, :2, :], (1, 3, 2, 0)).reshape(s, 2 * nh)
    dba = jnp.pad(dba, ((0, 0), (0, LANE - 2 * nh)))
    t = _mm("mm_dba", dba, wl["w_ba"], "nt", res=dr1, res_scale=alpha)
    dx = _mm("mm_dz", dz, wl["w_main"], "nt", res=t)
    gw_main = _mm("mm_gw_main", xb, dz, "tn")
    gw_ba = _mm("mm_gw_ba", xb, dba, "tn")
    g["w_in"] = jnp.concatenate([gw_main[:, :6 * d], gw_ba[:, :2 * nh], gw_main[:, 6 * d:]], axis=1)
    dbz = _col_sums("colsum_dz", dz)
    dbb = _col_sums("colsum_dba", dba)
    g["b_in"] = jnp.concatenate([dbz[:, :6 * d], dbb[:, :2 * nh], dbz[:, 6 * d:]], axis=1)
    return dx, g, side_out


def kernel(x, w_in, b_in, conv_dw_w, conv_dw_b, conv_ln_g, conv_ln_b, w_conv_proj, b_conv_proj, short_conv_w, a_log, dt_bias, gdn_norm_w, w_gdn_proj, w_out, ln1_g, ln1_b, w_ffn_in, w_ffn_out, ln2_g, ln2_b, loss_target, m_w_in, m_b_in, m_conv_dw_w, m_conv_dw_b, m_conv_ln_g, m_conv_ln_b, m_w_conv_proj, m_b_conv_proj, m_short_conv_w, m_a_log, m_dt_bias, m_gdn_norm_w, m_w_gdn_proj, m_w_out, m_ln1_g, m_ln1_b, m_w_ffn_in, m_w_ffn_out, m_ln2_g, m_ln2_b, v_w_in, v_b_in, v_conv_dw_w, v_conv_dw_b, v_conv_ln_g, v_conv_ln_b, v_w_conv_proj, v_b_conv_proj, v_short_conv_w, v_a_log, v_dt_bias, v_gdn_norm_w, v_w_gdn_proj, v_w_out, v_ln1_g, v_ln1_b, v_w_ffn_in, v_w_ffn_out, v_ln2_g, v_ln2_b):
    wts = dict(w_in=w_in, b_in=b_in, conv_dw_w=conv_dw_w, conv_dw_b=conv_dw_b, conv_ln_g=conv_ln_g, conv_ln_b=conv_ln_b,
               w_conv_proj=w_conv_proj, b_conv_proj=b_conv_proj, short_conv_w=short_conv_w, a_log=a_log, dt_bias=dt_bias,
               gdn_norm_w=gdn_norm_w, w_gdn_proj=w_gdn_proj, w_out=w_out, ln1_g=ln1_g, ln1_b=ln1_b, w_ffn_in=w_ffn_in,
               w_ffn_out=w_ffn_out, ln2_g=ln2_g, ln2_b=ln2_b)
    mom = dict(w_in=m_w_in, b_in=m_b_in, conv_dw_w=m_conv_dw_w, conv_dw_b=m_conv_dw_b, conv_ln_g=m_conv_ln_g,
               conv_ln_b=m_conv_ln_b, w_conv_proj=m_w_conv_proj, b_conv_proj=m_b_conv_proj, short_conv_w=m_short_conv_w,
               a_log=m_a_log, dt_bias=m_dt_bias, gdn_norm_w=m_gdn_norm_w, w_gdn_proj=m_w_gdn_proj, w_out=m_w_out,
               ln1_g=m_ln1_g, ln1_b=m_ln1_b, w_ffn_in=m_w_ffn_in, w_ffn_out=m_w_ffn_out, ln2_g=m_ln2_g, ln2_b=m_ln2_b)
    var = dict(w_in=v_w_in, b_in=v_b_in, conv_dw_w=v_conv_dw_w, conv_dw_b=v_conv_dw_b, conv_ln_g=v_conv_ln_g,
               conv_ln_b=v_conv_ln_b, w_conv_proj=v_w_conv_proj, b_conv_proj=v_b_conv_proj, short_conv_w=v_short_conv_w,
               a_log=v_a_log, dt_bias=v_dt_bias, gdn_norm_w=v_gdn_norm_w, w_gdn_proj=v_w_gdn_proj, w_out=v_w_out,
               ln1_g=v_ln1_g, ln1_b=v_ln1_b, w_ffn_in=v_w_ffn_in, w_ffn_out=v_w_ffn_out, ln2_g=v_ln2_g, ln2_b=v_ln2_b)

    depth = w_in.shape[0]
    _, s, d = x.shape
    nh = d // HEAD
    alpha = float((2.0 * depth) ** 0.25)
    xi, yi, ci = _place()
    chip = 2 * xi + yi
    c_arr = jnp.reshape(ci, (1,)).astype(jnp.int32)
    idx_arr = jnp.stack([chip, ci]).astype(jnp.int32)

    shard_shapes = [wts[n].shape[1:] for n in BIG]
    fw = FLAT_W if all(int(np.prod(sh)) % FLAT_W == 0 for sh in shard_shapes) else LANE
    rows, total_rows, pad_rows = _flat_rows(shard_shapes, fw)

    kw, cs = conv_dw_w.shape[1], conv_dw_w.shape[2]
    ks, ss = short_conv_w.shape[1], short_conv_w.shape[2]
    small_w = jnp.concatenate([conv_dw_w.reshape(depth * kw, cs), jnp.zeros(((-depth * kw) % 8, cs), F32)], axis=0)
    cw_all = _gather_chips(small_w)[:, :depth * kw].reshape(4, depth, kw, cs)
    cw_all = jnp.transpose(cw_all, (1, 2, 0, 3)).reshape(depth, kw, 4 * cs)
    small_s = jnp.concatenate([short_conv_w.reshape(depth * ks, ss), jnp.zeros(((-depth * ks) % 8, ss), F32)], axis=0)
    sw_all = _gather_chips(small_s)[:, :depth * ks].reshape(4, depth, ks, ss)
    sw_all = jnp.transpose(sw_all, (1, 2, 0, 3)).reshape(depth, ks, 4 * ss)

    def my_flat(l):
        parts = [wts[n][l].reshape(-1, fw) for n in BIG]
        if pad_rows:
            parts.append(jnp.zeros((pad_rows, fw), F32))
        return jnp.concatenate(parts, axis=0).astype(BF16)

    def layer_weights(l, others):
        gathered = lax.dynamic_update_slice(others, my_flat(l)[None], (chip, 0, 0))
        full, off = {}, 0
        for n, sh, nr in zip(BIG, shard_shapes, rows):
            blk = gathered[:, off:off + nr, :].reshape((4,) + tuple(sh))
            off += nr
            if n in COL_SHARDED:
                full[n] = jnp.transpose(blk, (1, 0, 2)).reshape(sh[0], 4 * sh[1])
            else:
                full[n] = blk.reshape(4 * sh[0], sh[1])
        wi = full["w_in"]
        bi = b_in[l]
        f_ff = full["w_ffn_in"].shape[1] // 2
        return dict(
            conv_dw_w=cw_all[l], short_conv_w=sw_all[l],
            w_main=jnp.concatenate([wi[:, :6 * d], wi[:, 6 * d + 2 * nh:]], axis=1),
            w_ba=jnp.pad(wi[:, 6 * d:6 * d + 2 * nh], ((0, 0), (0, LANE - 2 * nh))),
            b_main=_row(jnp.concatenate([bi[:6 * d], bi[6 * d + 2 * nh:]])),
            b_ba=_row(jnp.pad(bi[6 * d:6 * d + 2 * nh], (0, LANE - 2 * nh))),
            w_conv_proj=full["w_conv_proj"], w_gdn_proj=full["w_gdn_proj"], w_out=full["w_out"],
            w_ffn_g=full["w_ffn_in"][:, :f_ff], w_ffn_u=full["w_ffn_in"][:, f_ff:], w_ffn_out=full["w_ffn_out"],
            b_conv_proj=_row(b_conv_proj[l]), zero_bias=jnp.zeros((1, d), F32),
            conv_dw_b=_row(conv_dw_b[l]), conv_ln_g=_row(conv_ln_g[l]), conv_ln_b=_row(conv_ln_b[l]),
            gdn_norm_w=_row(gdn_norm_w[l]), ln1_g=_row(ln1_g[l]), ln1_b=_row(ln1_b[l]), ln2_g=_row(ln2_g[l]), ln2_b=_row(ln2_b[l]),
            hp=jnp.concatenate([jnp.broadcast_to(a_log[l][:, None, None], (nh, 1, LANE)),
                                jnp.broadcast_to(dt_bias[l][:, None, None], (nh, 1, LANE)),
                                jnp.zeros((nh, 6, LANE), F32)], axis=1),
        )

    h = x[0]
    hb16 = h.astype(BF16)
    saved, layers = [], []
    others = _run_side("gather_split", _gather_side(my_flat(0)))[0]
    for l in range(depth):
        layers.append(layer_weights(l, others))
        side = _gather_side(my_flat(l + 1)) if l + 1 < depth else None
        h, hb16, sv, side_out = _layer_fwd(h, hb16, layers[l], alpha, side)
        if side is not None:
            others = side_out[0]
        saved.append(sv)
    loss_blk, dy = _loss_fwd_bwd(h, loss_target[0])
    loss = lax.psum(0.5 * loss_blk[0, 0], ("x", "y", "c"))

    grads = [None] * depth
    shard_grads = [None] * depth

    def finish(l, pair, got):
        shard = _join_halves(_add_chips(pair, got, idx_arr))
        sg, off = {}, 0
        for n, sh, nr in zip(BIG, shard_shapes, rows):
            sg[n] = shard[off:off + nr].reshape(sh)
            off += nr
        shard_grads[l] = sg

    pending = None
    for l in reversed(range(depth)):
        side = _scatter_side(pending[1]) if pending is not None else None
        dy, g, side_out = _layer_bwd(dy, saved[l], layers[l], alpha, side)
        if pending is not None:
            finish(pending[0], pending[1], side_out[0])
        grads[l] = g
        parts = []
        for n, sh in zip(BIG, shard_shapes):
            gf = g[n]
            if n in COL_SHARDED:
                gs = jnp.transpose(gf.reshape(sh[0], 4, sh[1]), (1, 0, 2))
            else:
                gs = gf.reshape(4, sh[0], sh[1])
            parts.append(gs.reshape(4, -1, fw))
        if pad_rows:
            parts.append(jnp.zeros((4, pad_rows, fw), F32))
        gflat = jnp.concatenate(parts, axis=1)
        pending = (l, _add_half(gflat, _swap_halves(gflat), c_arr))
    finish(pending[0], pending[1], _run_side("scatter_chips", _scatter_side(pending[1]))[0])
    grad_x = dy[None]

    small_parts = [jnp.concatenate([grads[l][n].reshape(-1) for l in range(depth)]) for n in SMALL]
    sizes = [int(p.shape[0]) for p in small_parts]
    flat = jnp.concatenate(small_parts)
    nflat = int(flat.shape[0])
    flat = jnp.pad(flat, (0, (-nflat) % (8 * LANE))).reshape(-1, LANE)
    tot = _sum8(_gather_all(flat)).reshape(-1)
    small_g, off = {}, 0
    for n, sz in zip(SMALL, sizes):
        full = tot[off:off + sz]
        off += sz
        if n == "conv_dw_w":
            full = lax.dynamic_slice_in_dim(full.reshape(depth, kw, 4 * cs), chip * cs, cs, axis=2)
        elif n == "short_conv_w":
            full = lax.dynamic_slice_in_dim(full.reshape(depth, ks, 4 * ss), chip * ss, ss, axis=2)
        small_g[n] = full.reshape(wts[n].shape)

    out_g, out_d, out_m, out_v = {}, {}, {}, {}
    for n in BIG:
        gfull = jnp.stack([shard_grads[l][n] for l in range(depth)])
        sh = wts[n].shape
        two = (sh[0] * sh[1], sh[2])
        dl, nm, nv = _adamw("adamw_" + n, wts[n].reshape(two), gfull.reshape(two), mom[n].reshape(two), var[n].reshape(two))
        out_g[n], out_d[n], out_m[n], out_v[n] = gfull, dl.reshape(sh), nm.reshape(sh), nv.reshape(sh)

    def pack(src):
        v = jnp.concatenate([src[n].reshape(-1) for n in SMALL])
        return jnp.pad(v, (0, (-int(v.shape[0])) % (8 * LANE))).reshape(-1, LANE)

    dl, nm, nv = _adamw("adamw_small", pack(wts), pack(small_g), pack(mom), pack(var))
    off = 0
    for n in SMALL:
        sz = int(np.prod(wts[n].shape))
        out_g[n] = small_g[n]
        out_d[n] = dl.reshape(-1)[off:off + sz].reshape(wts[n].shape)
        out_m[n] = nm.reshape(-1)[off:off + sz].reshape(wts[n].shape)
        out_v[n] = nv.reshape(-1)[off:off + sz].reshape(wts[n].shape)
        off += sz

    return (loss, grad_x, *[out_g[n] for n in ORDER], *[out_d[n] for n in ORDER], *[out_m[n] for n in ORDER],
            *[out_v[n] for n in ORDER])
```

```python
import functools

import jax
import jax.numpy as jnp
import numpy as np
from jax import lax
from jax.experimental import pallas as pl
from jax.experimental.pallas import tpu as pltpu

F32 = jnp.float32
BF16 = jnp.bfloat16
MESH = pl.DeviceIdType.MESH

LN_EPS = 1e-5
L2_EPS = 1e-6
CHUNK = 64
HEAD = 128
CONV_HALO = 32
SHORT_HALO = 8
ROW_BLOCK = 256
CONV_ROWS = 64
VMEM_LIMIT = 56 * 1024 * 1024
LANE = 128
FLAT_W = 1024

ADAM_LR, ADAM_B1, ADAM_B2, ADAM_EPS, ADAM_WD, ADAM_STEP = 0.001, 0.9, 0.999, 1e-08, 0.01, 10

NN = ((1,), (0,))
NT = ((1,), (1,))
TN = ((0,), (0,))


def _params(n_axes):
    return pltpu.CompilerParams(dimension_semantics=("arbitrary",) * n_axes, vmem_limit_bytes=VMEM_LIMIT)


def _tile(dim, pref, unit=LANE):
    if dim <= pref:
        return dim
    best = None
    for t in range(unit, pref + 1, unit):
        if dim % t == 0:
            best = t
    assert best is not None, (dim, pref, unit)
    return best


def _dotb(a, b, dims):
    return lax.dot_general(a.astype(BF16), b.astype(BF16), (dims, ((), ())), preferred_element_type=F32)


def _split(a):
    hi = a.astype(BF16)
    return hi, (a - hi.astype(F32)).astype(BF16)


def _sig(x):
    return jax.nn.sigmoid(x)


def _dsilu(x):
    s = _sig(x)
    return s * (1.0 + x * (1.0 - s))


def _softplus(x):
    return jnp.maximum(x, 0.0) + jnp.log(1.0 + jnp.exp(-jnp.abs(x)))


def _iota(shape, dim):
    return lax.broadcasted_iota(jnp.int32, shape, dim)


def _accum(ref, val, first):
    @pl.when(first)
    def _():
        ref[...] = val

    @pl.when(jnp.logical_not(first))
    def _():
        ref[...] += val


def _mm(name, a, b, mode, out_dtype=F32, bias=None, res=None, res_scale=1.0, tm=1024, tn=1024, tk=2048):
    if mode == "nn":
        (m, k), (k2, n) = a.shape, b.shape
    elif mode == "tn":
        (k, m), (k2, n) = a.shape, b.shape
    else:
        (m, k), (n, k2) = a.shape, b.shape
    assert k == k2, (name, a.shape, b.shape)
    tm, tn, tk = _tile(m, tm), _tile(n, tn), _tile(k, tk)
    nk = k // tk
    dims = {"nn": NN, "tn": TN, "nt": NT}[mode]
    a_spec = {"nn": pl.BlockSpec((tm, tk), lambda i, j, kk: (i, kk)),
              "tn": pl.BlockSpec((tk, tm), lambda i, j, kk: (kk, i)),
              "nt": pl.BlockSpec((tm, tk), lambda i, j, kk: (i, kk))}[mode]
    b_spec = {"nn": pl.BlockSpec((tk, tn), lambda i, j, kk: (kk, j)),
              "tn": pl.BlockSpec((tk, tn), lambda i, j, kk: (kk, j)),
              "nt": pl.BlockSpec((tn, tk), lambda i, j, kk: (j, kk))}[mode]
    ins, in_specs = [a, b], [a_spec, b_spec]
    if bias is not None:
        ins.append(bias)
        in_specs.append(pl.BlockSpec((1, tn), lambda i, j, kk: (0, j)))
    if res is not None:
        ins.append(res)
        in_specs.append(pl.BlockSpec((tm, tn), lambda i, j, kk: (i, j)))
    has_bias, has_res = bias is not None, res is not None

    def body(*refs):
        a_ref, b_ref = refs[0], refs[1]
        pos = 2
        bias_ref = res_ref = None
        if has_bias:
            bias_ref = refs[pos]
            pos += 1
        if has_res:
            res_ref = refs[pos]
            pos += 1
        o_ref = refs[pos]
        part = _dotb(a_ref[...], b_ref[...], dims)

        def finish(r):
            if has_bias:
                r = r + bias_ref[...]
            if has_res:
                r = r + res_scale * res_ref[...]
            o_ref[...] = r.astype(out_dtype)

        if nk == 1:
            finish(part)
        else:
            acc_ref = refs[pos + 1]
            kk = pl.program_id(2)
            _accum(acc_ref, part, kk == 0)

            @pl.when(kk == nk - 1)
            def _():
                finish(acc_ref[...])

    return pl.pallas_call(
        body, name=name, grid=(m // tm, n // tn, nk), in_specs=in_specs,
        out_specs=pl.BlockSpec((tm, tn), lambda i, j, kk: (i, j)),
        out_shape=jax.ShapeDtypeStruct((m, n), out_dtype),
        scratch_shapes=[pltpu.VMEM((tm, tn), F32)] if nk > 1 else [], compiler_params=_params(3),
    )(*ins)


def _ew(name, fn, grid, ins, in_specs, out_shapes, out_specs, scratch=(), side=None, fill=None):
    s_ins, s_outs, s_sems, s_start, s_wait = side if side is not None else ((), (), (), None, None)
    n_in, n_out, n_si, n_so, n_scr = len(ins), len(out_shapes), len(s_ins), len(s_outs), len(scratch)
    n_fill = 0 if fill is None else 1

    def body(*refs):
        pos = [0]

        def take(n):
            part = refs[pos[0]:pos[0] + n]
            pos[0] += n
            return part

        i_refs, _filled, si_refs = take(n_in), take(n_fill), take(n_si)
        o_refs, so_refs, scr, sems = take(n_out), take(n_so), take(n_scr), take(len(s_sems))
        if side is not None:
            ids = [pl.program_id(a) for a in range(len(grid))]
            first = functools.reduce(jnp.logical_and, [p == 0 for p in ids])
            last = functools.reduce(jnp.logical_and, [p == g - 1 for p, g in zip(ids, grid)])

            @pl.when(first)
            def _():
                s_start(si_refs, so_refs, sems)

        fn(i_refs, o_refs, scr)
        if side is not None:
            @pl.when(last)
            def _():
                s_wait(si_refs, so_refs, sems)

    out = pl.pallas_call(
        body, name=name, grid=grid, in_specs=list(in_specs) + [ANY] * (n_fill + n_si),
        out_specs=list(out_specs) + [ANY] * n_so, out_shape=list(out_shapes) + list(s_outs),
        scratch_shapes=list(scratch) + list(s_sems), compiler_params=_params(len(grid)),
        input_output_aliases={} if fill is None else {n_in: fill[1]},
    )(*ins, *([] if fill is None else [fill[0]]), *s_ins)
    return out


def _sds(shape, dtype=F32):
    return jax.ShapeDtypeStruct(tuple(shape), dtype)


def _rows(bs, w, col=0):
    return pl.BlockSpec((bs, w), lambda i, col=col: (i, col))


def _par(r, w):
    return pl.BlockSpec((r, w), lambda i: (0, 0))


def _ln_stats(r):
    mu = jnp.mean(r, axis=-1, keepdims=True)
    xc = r - mu
    var = jnp.mean(xc * xc, axis=-1, keepdims=True)
    rstd = lax.rsqrt(var + LN_EPS)
    return xc * rstd, rstd


def _ln_back(xh, rstd, g, dy):
    dxh = dy * g
    m1 = jnp.mean(dxh, axis=-1, keepdims=True)
    m2 = jnp.mean(dxh * xh, axis=-1, keepdims=True)
    return rstd * (dxh - m1 - xh * m2)


def _colsum(v):
    return jnp.sum(v, axis=0, keepdims=True)


def _ln_res_fwd(x, sub, g, b, alpha):
    s, d = x.shape
    bs = min(ROW_BLOCK, s)

    def fn(i_refs, o_refs, _):
        x_ref, s_ref, g_ref, b_ref = i_refs
        xh, _r = _ln_stats(alpha * x_ref[...] + s_ref[...])
        y = xh * g_ref[...] + b_ref[...]
        o_refs[0][...] = y
        o_refs[1][...] = y.astype(BF16)

    return _ew("ln_res_fwd", fn, (s // bs,), [x, sub, g, b], [_rows(bs, d), _rows(bs, d), _par(1, d), _par(1, d)],
               [_sds((s, d)), _sds((s, d), BF16)], [_rows(bs, d), _rows(bs, d)])


def _ln_res_bwd(x, sub, g, dy, alpha):
    s, d = x.shape
    bs = min(ROW_BLOCK, s)

    def fn(i_refs, o_refs, _):
        x_ref, s_ref, g_ref, dy_ref = i_refs
        dr_ref, drb_ref, dg_ref, db_ref = o_refs
        first = pl.program_id(0) == 0
        xh, rstd = _ln_stats(alpha * x_ref[...] + s_ref[...])
        dy_v = dy_ref[...]
        dr = _ln_back(xh, rstd, g_ref[...], dy_v)
        dr_ref[...] = dr
        drb_ref[...] = dr.astype(BF16)
        _accum(dg_ref, _colsum(dy_v * xh), first)
        _accum(db_ref, _colsum(dy_v), first)

    return _ew("ln_res_bwd", fn, (s // bs,), [x, sub, g, dy], [_rows(bs, d), _rows(bs, d), _par(1, d), _rows(bs, d)],
               [_sds((s, d)), _sds((s, d), BF16), _sds((1, d)), _sds((1, d))],
               [_rows(bs, d), _rows(bs, d), _par(1, d), _par(1, d)])


def _loss_fwd_bwd(y, t):
    s, d = y.shape
    bs = min(ROW_BLOCK, s)

    def fn(i_refs, o_refs, _):
        err = i_refs[0][...] - i_refs[1][...]
        o_refs[1][...] = err * (1.0 / d)
        tot = jnp.sum(jnp.sum(err * err, axis=1, keepdims=True), axis=0, keepdims=True) * (1.0 / d)
        _accum(o_refs[0], jnp.broadcast_to(tot, (8, LANE)), pl.program_id(0) == 0)

    return _ew("loss", fn, (s // bs,), [y, t], [_rows(bs, d), _rows(bs, d)],
               [_sds((8, LANE)), _sds((s, d))], [_par(8, LANE), _rows(bs, d)])


def _conv_fwd(z, cw, cb, g, b, d):
    s = z.shape[0]
    kw = cw.shape[0]
    bs = min(ROW_BLOCK, s)
    hb = CONV_HALO
    r = bs // hb

    def prev(col):
        return pl.BlockSpec((hb, d), lambda i: (jnp.maximum(i * r - 1, 0), col))

    def fn(i_refs, o_refs, scr):
        a_ref, b_ref, ap_ref, bp_ref, w_ref, cb_ref, g_ref, be_ref = i_refs
        c3_ref, c1_ref = o_refs
        ext = scr[0]
        i = pl.program_id(0)
        ext[pl.ds(0, hb), :] = jnp.where(i > 0, ap_ref[...] * _sig(bp_ref[...]), 0.0)
        ext[pl.ds(hb, bs), :] = a_ref[...] * _sig(b_ref[...])
        for st in range(d // LANE):
            sl = pl.ds(st * LANE, LANE)
            for rb in range(bs // CONV_ROWS):
                acc = jnp.zeros((CONV_ROWS, LANE), F32) + cb_ref[:, sl]
                for j in range(kw):
                    acc = acc + w_ref[pl.ds(j, 1), sl] * ext[pl.ds(rb * CONV_ROWS + hb - (kw - 1) + j, CONV_ROWS), sl]
                c1_ref[pl.ds(rb * CONV_ROWS, CONV_ROWS), sl] = acc
        xh, _r = _ln_stats(c1_ref[...])
        c2 = xh * g_ref[...] + be_ref[...]
        c3_ref[...] = (c2 * _sig(c2)).astype(BF16)

    return _ew("conv_fwd", fn, (s // bs,), [z, z, z, z, cw, cb, g, b],
               [_rows(bs, d, 0), _rows(bs, d, 1), prev(0), prev(1), _par(kw, d), _par(1, d), _par(1, d), _par(1, d)],
               [_sds((s, d), BF16), _sds((s, d))], [_rows(bs, d), _rows(bs, d)], scratch=[pltpu.VMEM((hb + bs, d), F32)])


def _conv_bwd(z, c1, dc3, cw, g, b, d, dz, side=None):
    s = z.shape[0]
    kw = cw.shape[0]
    bs = min(ROW_BLOCK, s)
    hb = CONV_HALO
    r = bs // hb
    nrow = s // bs
    last_h = s // hb - 1

    def prev(col):
        return pl.BlockSpec((hb, d), lambda i: (jnp.maximum(i * r - 1, 0), col))

    def nxt(col):
        return pl.BlockSpec((hb, d), lambda i: (jnp.minimum((i + 1) * r, last_h), col))

    def fn(i_refs, o_refs, scr):
        a_ref, b_ref, ap_ref, bp_ref, c_ref, cn_ref, d_ref, dn_ref, w_ref, g_ref, be_ref = i_refs
        dz_ref, dw_ref, dcb_ref, dg_ref, dbe_ref, dbz_ref = o_refs
        ext, extd = scr
        i = pl.program_id(0)
        first = i == 0
        more = i < nrow - 1
        ext[pl.ds(0, hb), :] = jnp.where(i > 0, ap_ref[...] * _sig(bp_ref[...]), 0.0)
        ext[pl.ds(hb, bs), :] = a_ref[...] * _sig(b_ref[...])
        c1_e = jnp.concatenate([c_ref[...], jnp.where(more, cn_ref[...], 0.0)], axis=0)
        xh, rstd = _ln_stats(c1_e)
        c2 = xh * g_ref[...] + be_ref[...]
        dc3_e = jnp.concatenate([d_ref[...], jnp.where(more, dn_ref[...], 0.0)], axis=0)
        dc2 = dc3_e * _dsilu(c2)
        dc1 = _ln_back(xh, rstd, g_ref[...], dc2)
        extd[...] = dc1
        _accum(dg_ref, _colsum((dc2 * xh)[:bs]), first)
        _accum(dbe_ref, _colsum(dc2[:bs]), first)
        _accum(dcb_ref, _colsum(dc1[:bs]), first)

        @pl.when(first)
        def _():
            dw_ref[...] = jnp.zeros_like(dw_ref)
            dbz_ref[...] = jnp.zeros_like(dbz_ref)

        for st in range(d // LANE):
            sl = pl.ds(st * LANE, LANE)
            sl_gate = pl.ds(d + st * LANE, LANE)
            for rb in range(bs // CONV_ROWS):
                r0 = rb * CONV_ROWS
                own = extd[pl.ds(r0, CONV_ROWS), sl]
                dc0 = jnp.zeros((CONV_ROWS, LANE), F32)
                for j in range(kw):
                    dc0 = dc0 + w_ref[pl.ds(j, 1), sl] * extd[pl.ds(r0 + kw - 1 - j, CONV_ROWS), sl]
                    dw_ref[pl.ds(j, 1), sl] += _colsum(own * ext[pl.ds(r0 + hb - (kw - 1) + j, CONV_ROWS), sl])
                rows = pl.ds(r0, CONV_ROWS)
                a_v, sb = a_ref[rows, sl], _sig(b_ref[rows, sl])
                da = dc0 * sb
                db = dc0 * a_v * sb * (1.0 - sb)
                dz_ref[rows, sl] = da.astype(BF16)
                dz_ref[rows, sl_gate] = db.astype(BF16)
                dbz_ref[:, sl] += _colsum(da)
                dbz_ref[:, sl_gate] += _colsum(db)

    return _ew("conv_bwd", fn, (nrow,), [z, z, z, z, c1, c1, dc3, dc3, cw, g, b],
               [_rows(bs, d, 0), _rows(bs, d, 1), prev(0), prev(1), _rows(bs, d), nxt(0), _rows(bs, d), nxt(0),
                _par(kw, d), _par(1, d), _par(1, d)],
               [_sds(dz.shape, BF16), _sds((kw, d)), _sds((1, d)), _sds((1, d)), _sds((1, d)), _sds((1, 2 * d))],
               [_rows(bs, 2 * d, 0), _par(kw, d), _par(1, d), _par(1, d), _par(1, d), _par(1, 2 * d)],
               scratch=[pltpu.VMEM((bs + hb, d), F32), pltpu.VMEM((bs + hb, d), F32)], side=side, fill=(dz, 0))


def _sconv_fwd(z, sw, d, col0):
    s = z.shape[0]
    kw = sw.shape[0]
    bs = min(ROW_BLOCK, s)
    hb = SHORT_HALO
    r = bs // hb

    def fn(i_refs, o_refs, scr):
        x_ref, xp_ref, w_ref = i_refs
        ext = scr[0]
        i = pl.program_id(1)
        ext[pl.ds(0, hb), :] = jnp.where(i > 0, xp_ref[...], 0.0)
        ext[pl.ds(hb, bs), :] = x_ref[...]
        acc = jnp.zeros((bs, d), F32)
        for j in range(kw):
            acc = acc + w_ref[pl.ds(j, 1), :] * ext[pl.ds(hb - (kw - 1) + j, bs), :]
        o_refs[0][0] = acc * _sig(acc)

    return _ew("sconv_fwd", fn, (3, s // bs), [z, z, sw],
               [pl.BlockSpec((bs, d), lambda sg, i: (i, col0 + sg)),
                pl.BlockSpec((hb, d), lambda sg, i: (jnp.maximum(i * r - 1, 0), col0 + sg)),
                pl.BlockSpec((kw, d), lambda sg, i: (0, sg))],
               [_sds((3, s, d))], [pl.BlockSpec((1, bs, d), lambda sg, i: (sg, i, 0))],
               scratch=[pltpu.VMEM((hb + bs, d), F32)])[0]


def _sconv_bwd(z, dy, sw, d, col0, dz):
    s = z.shape[0]
    kw = sw.shape[0]
    bs = min(ROW_BLOCK, s)
    hb = SHORT_HALO
    r = bs // hb
    nrow = s // bs
    last_h = s // hb - 1

    def fn(i_refs, o_refs, scr):
        x_ref, xp_ref, xn_ref, d_ref, dn_ref, w_ref = i_refs
        dx_ref, dw_ref, dbz_ref = o_refs
        ext, extd = scr
        i = pl.program_id(1)
        more = i < nrow - 1
        ext[pl.ds(0, hb), :] = jnp.where(i > 0, xp_ref[...], 0.0)
        ext[pl.ds(hb, bs), :] = x_ref[...]
        ext[pl.ds(hb + bs, hb), :] = jnp.where(more, xn_ref[...], 0.0)
        n1 = bs + hb
        pre = jnp.zeros((n1, d), F32)
        for j in range(kw):
            pre = pre + w_ref[pl.ds(j, 1), :] * ext[pl.ds(hb - (kw - 1) + j, n1), :]
        dy_e = jnp.concatenate([d_ref[0], jnp.where(more, dn_ref[0], 0.0)], axis=0)
        extd[...] = dy_e * _dsilu(pre)

        @pl.when(i == 0)
        def _():
            dw_ref[...] = jnp.zeros_like(dw_ref)

        dp_own = extd[pl.ds(0, bs), :]
        dx = jnp.zeros((bs, d), F32)
        for j in range(kw):
            dx = dx + w_ref[pl.ds(j, 1), :] * extd[pl.ds(kw - 1 - j, bs), :]
            dw_ref[pl.ds(j, 1), :] += _colsum(dp_own * ext[pl.ds(hb - (kw - 1) + j, bs), :])
        dx_ref[...] = dx.astype(BF16)
        _accum(dbz_ref, _colsum(dx), i == 0)

    return _ew("sconv_bwd", fn, (3, nrow), [z, z, z, dy, dy, sw],
               [pl.BlockSpec((bs, d), lambda sg, i: (i, col0 + sg)),
                pl.BlockSpec((hb, d), lambda sg, i: (jnp.maximum(i * r - 1, 0), col0 + sg)),
                pl.BlockSpec((hb, d), lambda sg, i: (jnp.minimum((i + 1) * r, last_h), col0 + sg)),
                pl.BlockSpec((1, bs, d), lambda sg, i: (sg, i, 0)),
                pl.BlockSpec((1, hb, d), lambda sg, i: (sg, jnp.minimum((i + 1) * r, last_h), 0)),
                pl.BlockSpec((kw, d), lambda sg, i: (0, sg))],
               [_sds(dz.shape, BF16), _sds((kw, 3 * d)), _sds((1, 3 * d))],
               [pl.BlockSpec((bs, d), lambda sg, i: (i, col0 + sg)), pl.BlockSpec((kw, d), lambda sg, i: (0, sg)),
                pl.BlockSpec((1, d), lambda sg, i: (0, sg))],
               scratch=[pltpu.VMEM((bs + 2 * hb, d), F32), pltpu.VMEM((bs + hb, d), F32)], fill=(dz, 0))


GDN_HEADS = 16

BNN = (((2,), (1,)), ((0,), (0,)))
BNT = (((2,), (2,)), ((0,), (0,)))
BTN = (((1,), (1,)), ((0,), (0,)))


def _bdot(a, b, dn):
    return lax.dot_general(a.astype(BF16), b.astype(BF16), dn, preferred_element_type=F32)


def _bdotf(a, b, dn):
    ah, al = _split(a)
    bh, bl = _split(b)

    def dot(u, v):
        return lax.dot_general(u, v, dn, preferred_element_type=F32)

    return dot(ah, bh) + (dot(ah, bl) + dot(al, bh))


def _gdn_pre(qkv_ref, ba_ref, bat_ref, hp_ref, hb):
    g0 = pl.program_id(0) * hb
    c = CHUNK

    def heads(part):
        return jnp.stack([qkv_ref[part, :, pl.ds(hh * HEAD, HEAD)] for hh in range(hb)], axis=0)

    qr, kr, v = heads(0), heads(1), heads(2)
    ba = ba_ref[...]
    bat = bat_ref[0]
    nh = bat.shape[0] // 2
    lane = _iota(ba.shape, 1)
    sub = _iota(bat.shape, 0)

    def col(off):
        return jnp.stack([jnp.sum(jnp.where(lane == g0 + hh + off, ba, 0.0), axis=1, keepdims=True) for hh in range(hb)], axis=0)

    def row(off):
        return jnp.stack([jnp.sum(jnp.where(sub == g0 + hh + off, bat, 0.0), axis=0, keepdims=True) for hh in range(hb)], axis=0)

    braw_c, araw_c, braw_r, araw_r = col(0), col(nh), row(0), row(nh)
    alog = jnp.max(hp_ref[:, pl.ds(0, 1), :], axis=2, keepdims=True)
    dtb = jnp.max(hp_ref[:, pl.ds(1, 1), :], axis=2, keepdims=True)
    nega = -jnp.exp(alog)
    beta_c, beta_r = _sig(braw_c), _sig(braw_r)
    la_c = nega * _softplus(araw_c + dtb)
    la_r = nega * _softplus(araw_r + dtb)
    i = _iota((c, c), 0)
    j = _iota((c, c), 1)
    g_c = jnp.sum(jnp.where(j <= i, la_r, 0.0), axis=2, keepdims=True)
    g_r = jnp.sum(jnp.where(i <= j, la_c, 0.0), axis=1, keepdims=True)
    g_last = jnp.sum(la_c, axis=1, keepdims=True)
    low = i >= j
    dec = jnp.where(low, jnp.exp(jnp.where(low, g_c - g_r, 0.0)), 0.0)
    rq = lax.rsqrt(jnp.sum(qr * qr, axis=2, keepdims=True) + L2_EPS)
    rk = lax.rsqrt(jnp.sum(kr * kr, axis=2, keepdims=True) + L2_EPS)
    q = qr * (rq * HEAD ** -0.5)
    k = kr * rk
    kb = k * beta_c
    lmat = jnp.where(i > j, _bdot(kb, k, BNT) * dec, 0.0)
    eye = jnp.where(i == j, 1.0, 0.0)
    tinv = eye - lmat
    pw = lmat
    for _ in range(int(np.log2(c)) - 1):
        pw = _bdotf(pw, pw, BNN)
        tinv = _bdotf(tinv, eye + pw, BNN)
    eg_c = jnp.exp(g_c)
    rhs_w = kb * eg_c
    sol = _bdotf(tinv, jnp.concatenate([v * beta_c, rhs_w], axis=2), BNN)
    attn = jnp.where(low, _bdot(q, k, BNT) * dec, 0.0)
    ekd = jnp.exp(g_last - g_c)
    return dict(qr=qr, kr=kr, v=v, rq=rq, rk=rk, q=q, k=k, kb=kb, beta_c=beta_c, beta_r=beta_r, la_r=la_r,
                araw_r=araw_r, dtb=dtb, nega=nega, g_c=g_c, g_last=g_last, dec=dec, lmat=lmat, tinv=tinv,
                eg_c=eg_c, rhs_w=rhs_w, sol=sol, u=sol[:, :, :HEAD], w=sol[:, :, HEAD:], attn=attn, q_dec=q * eg_c,
                ekd=ekd, k_dec=k * ekd, i=i, j=j, low=low)


def _gdn_fwd(qkv, ba, bat, hp, side=None):
    _, s, d = qkv.shape
    nh, n, c = d // HEAD, s // CHUNK, CHUNK
    hb = min(GDN_HEADS, nh)
    gw = hb * HEAD

    def fn(i_refs, o_refs, scr):
        o_ref, st_ref = o_refs
        s_scr = scr[0]
        st = jnp.where(pl.program_id(1) == 0, 0.0, s_scr[...])
        p = _gdn_pre(*i_refs, hb)
        vn = p["u"] - _bdot(p["w"], st, BNN)
        o = _bdot(p["q_dec"], st, BNN) + _bdot(p["attn"], vn, BNN)
        s_scr[...] = st * jnp.exp(p["g_last"]) + _bdot(p["k_dec"], vn, BTN)
        st_ref[:, 0] = st
        for hh in range(hb):
            o_ref[:, pl.ds(hh * HEAD, HEAD)] = o[hh]

    return _ew("gdn_fwd", fn, (nh // hb, n), [qkv, ba, bat, hp],
               [pl.BlockSpec((3, c, gw), lambda h, t: (0, t, h)), pl.BlockSpec((c, LANE), lambda h, t: (t, 0)),
                pl.BlockSpec((1, bat.shape[1], c), lambda h, t: (t, 0, 0)), pl.BlockSpec((hb, 8, LANE), lambda h, t: (h, 0, 0))],
               [_sds((s, d)), _sds((nh, n, HEAD, HEAD))],
               [pl.BlockSpec((c, gw), lambda h, t: (t, h)), pl.BlockSpec((hb, 1, HEAD, HEAD), lambda h, t: (h, t, 0, 0))],
               scratch=[pltpu.VMEM((hb, HEAD, HEAD), F32)], side=side)


def _gdn_bwd(qkv, ba, bat, hp, states, do):
    _, s, d = qkv.shape
    nh, n, c = d // HEAD, s // CHUNK, CHUNK
    hb = min(GDN_HEADS, nh)
    gw = hb * HEAD

    def fn(i_refs, o_refs, scr):
        qkv_ref, ba_ref, bat_ref, hp_ref, st_ref, do_ref = i_refs
        dqkv_ref, dbat_ref, dhp_ref = o_refs
        ds_scr = scr[0]
        first = pl.program_id(1) == 0
        ds1 = jnp.where(first, 0.0, ds_scr[...])
        dhp_old = jnp.where(first, 0.0, dhp_ref[...])
        st = st_ref[:, 0]
        do_v = jnp.stack([do_ref[:, pl.ds(hh * HEAD, HEAD)] for hh in range(hb)], axis=0)
        p = _gdn_pre(qkv_ref, ba_ref, bat_ref, hp_ref, hb)
        i, j, low = p["i"], p["j"], p["low"]
        u, w, sol, attn, dec = p["u"], p["w"], p["sol"], p["attn"], p["dec"]
        q, k, kb, v = p["q"], p["k"], p["kb"], p["v"]
        eg_c, ekd, beta_c = p["eg_c"], p["ekd"], p["beta_c"]
        egl = jnp.exp(p["g_last"])

        def rsum(a):
            return jnp.sum(a, axis=2, keepdims=True)

        def csum(a):
            return jnp.sum(a, axis=1, keepdims=True)

        vn = u - _bdot(w, st, BNN)
        dvn = _bdot(attn, do_v, BTN) + _bdot(p["k_dec"], ds1, BNN)
        dattn = jnp.where(low, _bdot(do_v, vn, BNT), 0.0)
        dqd = _bdot(do_v, st, BNT)
        dkd = _bdot(vn, ds1, BNT)
        ds_scr[...] = _bdot(p["q_dec"], do_v, BTN) + egl * ds1 - _bdot(w, dvn, BTN)
        dgl = csum(rsum(st * ds1)) * egl
        dw = -_bdot(dvn, st, BNT)
        drhs = _bdotf(p["tinv"], jnp.concatenate([dvn, dw], axis=2), BTN)
        da = -jnp.where(i > j, _bdot(drhs, sol, BNT), 0.0)
        mm = da * p["lmat"] + dattn * attn
        dg_c = rsum(mm)
        dg_r = -csum(mm)
        dkk = da * dec
        dqk = dattn * dec
        dkb = _bdot(dkk, k, BNN)
        dk = _bdot(dkk, kb, BTN) + _bdot(dqk, q, BTN)
        dq = _bdot(dqk, k, BNN) + dqd * eg_c
        dg_c = dg_c + rsum(dqd * p["q_dec"])
        dk = dk + dkd * ekd
        t = rsum(dkd * p["k_dec"])
        dgl = dgl + csum(t)
        dg_c = dg_c - t
        drhs_u, drhs_w = drhs[:, :, :HEAD], drhs[:, :, HEAD:]
        dkb = dkb + drhs_w * eg_c
        dg_c = dg_c + rsum(drhs_w * p["rhs_w"])
        dv_out = drhs_u * beta_c
        dbeta_c = rsum(drhs_u * v) + rsum(dkb * k)
        dk = dk + dkb * beta_c
        diag = i == j
        dg = dg_c + rsum(jnp.where(diag, dg_r, 0.0))
        dla_r = csum(jnp.where(low, dg, 0.0)) + dgl
        dbeta_r = csum(jnp.where(diag, dbeta_c, 0.0))
        beta_r = p["beta_r"]
        dbraw_r = dbeta_r * beta_r * (1.0 - beta_r)
        daraw_r = dla_r * p["nega"] * _sig(p["araw_r"] + p["dtb"])
        dalog = rsum(dla_r * p["la_r"])
        ddtb = rsum(daraw_r)
        row8 = _iota((8, c), 0)
        dbat_ref[:, 0] = jnp.where(row8 == 0, dbraw_r, jnp.where(row8 == 1, daraw_r, 0.0))
        rowp = _iota((8, LANE), 0)
        dhp_ref[...] = dhp_old + jnp.where(rowp == 0, dalog, jnp.where(rowp == 1, ddtb, 0.0))
        qr, kr, rq, rk = p["qr"], p["kr"], p["rq"], p["rk"]
        sc = HEAD ** -0.5
        dq_out = sc * (rq * dq - qr * (rq * rq * rq * rsum(dq * qr)))
        dk_out = rk * dk - kr * (rk * rk * rk * rsum(dk * kr))
        for hh in range(hb):
            sl = pl.ds(hh * HEAD, HEAD)
            dqkv_ref[0, :, sl] = dq_out[hh]
            dqkv_ref[1, :, sl] = dk_out[hh]
            dqkv_ref[2, :, sl] = dv_out[hh]

    rev = n - 1
    return _ew("gdn_bwd", fn, (nh // hb, n), [qkv, ba, bat, hp, states, do],
               [pl.BlockSpec((3, c, gw), lambda h, t: (0, rev - t, h)), pl.BlockSpec((c, LANE), lambda h, t: (rev - t, 0)),
                pl.BlockSpec((1, bat.shape[1], c), lambda h, t: (rev - t, 0, 0)), pl.BlockSpec((hb, 8, LANE), lambda h, t: (h, 0, 0)),
                pl.BlockSpec((hb, 1, HEAD, HEAD), lambda h, t: (h, rev - t, 0, 0)), pl.BlockSpec((c, gw), lambda h, t: (rev - t, h))],
               [_sds((3, s, d)), _sds((nh, n, 8, c)), _sds((nh, 8, LANE))],
               [pl.BlockSpec((3, c, gw), lambda h, t: (0, rev - t, h)), pl.BlockSpec((hb, 1, 8, c), lambda h, t: (h, rev - t, 0, 0)),
                pl.BlockSpec((hb, 8, LANE), lambda h, t: (h, 0, 0))],
               scratch=[pltpu.VMEM((hb, HEAD, HEAD), F32)])


def _gdn_post_fwd(o, z, gw, d, zcol):
    s = o.shape[0]
    bs = min(ROW_BLOCK, s)

    def fn(i_refs, o_refs, _):
        o_ref, z_ref, w_ref = i_refs
        wv = w_ref[...]
        for h in range(d // HEAD):
            sl = pl.ds(h * HEAD, HEAD)
            oh, zg = o_ref[:, sl], z_ref[:, sl]
            r = lax.rsqrt(jnp.mean(oh * oh, axis=1, keepdims=True) + L2_EPS)
            o_refs[0][:, sl] = (oh * r * wv * (zg * _sig(zg))).astype(BF16)

    return _ew("gdn_post_fwd", fn, (s // bs,), [o, z, gw], [_rows(bs, d), _rows(bs, d, zcol), _par(1, HEAD)],
               [_sds((s, d), BF16)], [_rows(bs, d)])[0]


def _gdn_post_bwd(dog, o, z, gw, d, zcol, dz):
    s = o.shape[0]
    bs = min(ROW_BLOCK, s)

    def fn(i_refs, o_refs, _):
        g_ref, o_ref, z_ref, w_ref = i_refs
        do_ref, dz_ref, dw_ref, dbz_ref = o_refs
        first = pl.program_id(0) == 0
        wv = w_ref[...]
        dwacc = jnp.zeros((1, HEAD), F32)
        dbz = []
        for h in range(d // HEAD):
            sl = pl.ds(h * HEAD, HEAD)
            oh, zg, gv = o_ref[:, sl], z_ref[:, sl], g_ref[:, sl]
            r = lax.rsqrt(jnp.mean(oh * oh, axis=1, keepdims=True) + L2_EPS)
            on = oh * r
            sil = zg * _sig(zg)
            dzg = gv * on * wv * _dsilu(zg)
            dz_ref[:, sl] = dzg.astype(BF16)
            dbz.append(_colsum(dzg))
            dwacc = dwacc + _colsum(gv * on * sil)
            don = gv * wv * sil
            do_ref[:, sl] = r * (don - on * jnp.mean(don * on, axis=1, keepdims=True))
        _accum(dw_ref, dwacc, first)
        _accum(dbz_ref, jnp.concatenate(dbz, axis=1), first)

    return _ew("gdn_post_bwd", fn, (s // bs,), [dog, o, z, gw],
               [_rows(bs, d), _rows(bs, d), _rows(bs, d, zcol), _par(1, HEAD)],
               [_sds((s, d)), _sds(dz.shape, BF16), _sds((1, HEAD)), _sds((1, d))],
               [_rows(bs, d), _rows(bs, d, zcol), _par(1, HEAD), _par(1, d)], fill=(dz, 1))


def _merge_fwd(z, yc, yg, d, col_a, col_b):
    s = z.shape[0]
    bs = min(ROW_BLOCK, s)

    def fn(i_refs, o_refs, _):
        ga, gb, yc_ref, yg_ref = i_refs
        o_refs[0][...] = (_sig(ga[...]) * yc_ref[...] + _sig(gb[...]) * yg_ref[...]).astype(BF16)

    return _ew("merge_fwd", fn, (s // bs,), [z, z, yc, yg],
               [_rows(bs, d, col_a), _rows(bs, d, col_b), _rows(bs, d), _rows(bs, d)], [_sds((s, d), BF16)], [_rows(bs, d)])[0]


def _merge_bwd(dm, z, yc, yg, d, col_a):
    s, zw = z.shape
    bs = min(ROW_BLOCK, s)

    def fn(i_refs, o_refs, _):
        dm_ref, ga, gb, yc_ref, yg_ref = i_refs
        dyc_ref, dyg_ref, dz_ref, dbc_ref, dbz_ref = o_refs
        first = pl.program_id(0) == 0
        dmv = dm_ref[...]
        sa, sb = _sig(ga[...]), _sig(gb[...])
        dyc = dmv * sa
        dyc_ref[...] = dyc.astype(BF16)
        dyg_ref[...] = (dmv * sb).astype(BF16)
        dga = dmv * yc_ref[...] * sa * (1.0 - sa)
        dgb = dmv * yg_ref[...] * sb * (1.0 - sb)
        dz_ref[:, pl.ds(0, d)] = dga.astype(BF16)
        dz_ref[:, pl.ds(d, d)] = dgb.astype(BF16)
        _accum(dbc_ref, _colsum(dyc), first)
        _accum(dbz_ref, jnp.concatenate([_colsum(dga), _colsum(dgb)], axis=1), first)

    return _ew("merge_bwd", fn, (s // bs,), [dm, z, z, yc, yg],
               [_rows(bs, d), _rows(bs, d, col_a), _rows(bs, d, col_a + 1), _rows(bs, d), _rows(bs, d)],
               [_sds((s, d), BF16), _sds((s, d), BF16), _sds((s, zw), BF16), _sds((1, d)), _sds((1, 2 * d))],
               [_rows(bs, d), _rows(bs, d), _rows(bs, 2 * d, col_a // 2), _par(1, d), _par(1, 2 * d)])


def _swiglu_fwd(hg, hu):
    s, f = hg.shape
    bs = min(ROW_BLOCK, s)
    cw = _tile(f, 2048)

    def fn(i_refs, o_refs, _):
        g = i_refs[0][...].astype(F32)
        o_refs[0][...] = (g * _sig(g) * i_refs[1][...].astype(F32)).astype(BF16)

    spec = pl.BlockSpec((bs, cw), lambda i, j: (i, j))
    return _ew("swiglu_fwd", fn, (s // bs, f // cw), [hg, hu], [spec, spec], [_sds((s, f), BF16)], [spec])[0]


def _swiglu_bwd(hg, hu, df, side=None):
    s, f = hg.shape
    bs = min(ROW_BLOCK, s)
    cw = _tile(f, 2048)

    def fn(i_refs, o_refs, _):
        g, u, dfv = i_refs[0][...].astype(F32), i_refs[1][...].astype(F32), i_refs[2][...]
        o_refs[0][...] = (dfv * u * _dsilu(g)).astype(BF16)
        o_refs[1][...] = (dfv * g * _sig(g)).astype(BF16)

    spec = pl.BlockSpec((bs, cw), lambda i, j: (i, j))
    return _ew("swiglu_bwd", fn, (s // bs, f // cw), [hg, hu, df], [spec] * 3, [_sds((s, f), BF16)] * 2, [spec] * 2,
               side=side)


def _col_sums(name, a):
    s, n = a.shape
    bs = min(ROW_BLOCK, s)
    cw = _tile(n, 2048)

    def fn(i_refs, o_refs, _):
        _accum(o_refs[0], _colsum(i_refs[0][...].astype(F32)), pl.program_id(1) == 0)

    return _ew(name, fn, (n // cw, s // bs), [a], [pl.BlockSpec((bs, cw), lambda j, i: (i, j))],
               [_sds((1, n))], [pl.BlockSpec((1, cw), lambda j, i: (0, j))])[0]


def _adamw(name, w, g, m, v):
    r, c = w.shape
    br = r
    if r * c * 4 > (1 << 20):
        cands = [t for t in range(8, r, 8) if r % t == 0 and t * c * 4 <= (1 << 20)]
        br = max(cands) if cands else r
    c1 = 1.0 - ADAM_B1 ** ADAM_STEP
    c2 = 1.0 - ADAM_B2 ** ADAM_STEP

    def fn(i_refs, o_refs, _):
        wv, gv, mv, vv = (x[...] for x in i_refs)
        m2 = ADAM_B1 * mv + (1.0 - ADAM_B1) * gv
        v2 = ADAM_B2 * vv + (1.0 - ADAM_B2) * (gv * gv)
        o_refs[0][...] = -ADAM_LR * ((m2 / c1) / (jnp.sqrt(v2 / c2) + ADAM_EPS) + ADAM_WD * wv)
        o_refs[1][...] = m2
        o_refs[2][...] = v2

    spec = pl.BlockSpec((br, c), lambda i: (i, 0))
    return _ew(name, fn, (r // br,), [w, g, m, v], [spec] * 4, [_sds((r, c))] * 3, [spec] * 3)


ANY = pl.BlockSpec(memory_space=pl.ANY)


def _place():
    return lax.axis_index("x"), lax.axis_index("y"), lax.axis_index("c")


def _gather_chips(flat):
    r, w = flat.shape

    def body(x_ref, o_ref, ssem, rsem, lsem):
        x, y, c = _place()
        me = 2 * x + y
        chips = [(1 - x, y), (x, 1 - y), (1 - x, 1 - y)]
        local = pltpu.make_async_copy(x_ref, o_ref.at[me], lsem)
        local.start()
        sends = []
        for j, (px, py) in enumerate(chips):
            cp = pltpu.make_async_remote_copy(src_ref=x_ref, dst_ref=o_ref.at[me], send_sem=ssem.at[j], recv_sem=rsem.at[j],
                                              device_id=(px, py, c), device_id_type=MESH)
            cp.start()
            sends.append(cp)
        for j, (px, py) in enumerate(chips):
            pltpu.make_async_remote_copy(src_ref=x_ref, dst_ref=o_ref.at[2 * px + py], send_sem=ssem.at[j], recv_sem=rsem.at[j],
                                         device_id=(px, py, c), device_id_type=MESH).wait_recv()
        for cp in sends:
            cp.wait_send()
        local.wait()

    return pl.pallas_call(
        body, name="gather_chips", in_specs=[ANY], out_specs=ANY, out_shape=_sds((4, r, w), flat.dtype),
        scratch_shapes=[pltpu.SemaphoreType.DMA((3,)), pltpu.SemaphoreType.DMA((3,)), pltpu.SemaphoreType.DMA(())],
    )(flat)


def _gather_all(flat):
    r, w = flat.shape
    masks = [(mx, my, mc) for mx in (0, 1) for my in (0, 1) for mc in (0, 1)][1:]

    def body(x_ref, o_ref, ssem, rsem, lsem):
        x, y, c = _place()
        me = 4 * x + 2 * y + c
        local = pltpu.make_async_copy(x_ref, o_ref.at[me], lsem)
        local.start()
        peers = [(x ^ mx, y ^ my, c ^ mc) for (mx, my, mc) in masks]
        sends = []
        for j, peer in enumerate(peers):
            cp = pltpu.make_async_remote_copy(src_ref=x_ref, dst_ref=o_ref.at[me], send_sem=ssem.at[j], recv_sem=rsem.at[j],
                                              device_id=peer, device_id_type=MESH)
            cp.start()
            sends.append(cp)
        for j, (px, py, pc) in enumerate(peers):
            pltpu.make_async_remote_copy(src_ref=x_ref, dst_ref=o_ref.at[4 * px + 2 * py + pc], send_sem=ssem.at[j],
                                         recv_sem=rsem.at[j], device_id=(px, py, pc), device_id_type=MESH).wait_recv()
        for cp in sends:
            cp.wait_send()
        local.wait()

    return pl.pallas_call(
        body, name="gather_all", in_specs=[ANY], out_specs=ANY, out_shape=_sds((8, r, w), flat.dtype),
        scratch_shapes=[pltpu.SemaphoreType.DMA((7,)), pltpu.SemaphoreType.DMA((7,)), pltpu.SemaphoreType.DMA(())],
    )(flat)


def _sum8(parts):
    _, r, w = parts.shape

    def fn(i_refs, o_refs, _):
        acc = i_refs[0][0]
        for k in range(1, 8):
            acc = acc + i_refs[0][k]
        o_refs[0][...] = acc

    br = _tile(r, 512, 8)
    return _ew("sum8", fn, (r // br,), [parts], [pl.BlockSpec((8, br, w), lambda i: (0, i, 0))], [_sds((r, w))],
               [pl.BlockSpec((br, w), lambda i: (i, 0))])[0]


def _swap_side(g):
    _, r, w = g.shape
    rh = r // 2

    def copy(g_ref, o_ref, ssem, rsem):
        x, y, c = _place()
        return pltpu.make_async_remote_copy(src_ref=g_ref.at[:, pl.ds((1 - c) * rh, rh), :], dst_ref=o_ref, send_sem=ssem,
                                            recv_sem=rsem, device_id=(x, y, 1 - c), device_id_type=MESH)

    def start(si, so, sems):
        copy(si[0], so[0], *sems).start()

    def wait(si, so, sems):
        copy(si[0], so[0], *sems).wait()

    return [g], [_sds((4, rh, w), g.dtype)], [pltpu.SemaphoreType.DMA(()), pltpu.SemaphoreType.DMA(())], start, wait


def _add_half(g, got, c_arr):
    _, r, w = g.shape
    rh = r // 2
    br = _tile(rh, 512, 16)
    nb = rh // br

    def body(c_ref, g_ref, r_ref, o_ref):
        o_ref[...] = (g_ref[...] + r_ref[...]).astype(BF16)

    return pl.pallas_call(
        body, name="add_half",
        grid_spec=pltpu.PrefetchScalarGridSpec(
            num_scalar_prefetch=1, grid=(4, nb),
            in_specs=[pl.BlockSpec((1, br, w), lambda s, i, c_ref: (s, c_ref[0] * nb + i, 0)),
                      pl.BlockSpec((1, br, w), lambda s, i, c_ref: (s, i, 0))],
            out_specs=pl.BlockSpec((1, br, w), lambda s, i, c_ref: (s, i, 0))),
        out_shape=_sds((4, rh, w), BF16), compiler_params=_params(2),
    )(c_arr, g, got)


def _scatter_side(p):
    _, rh, w = p.shape

    def copies(p_ref, o_ref, ssem, rsem):
        x, y, c = _place()
        chips = [(1 - x, y), (x, 1 - y), (1 - x, 1 - y)]
        return [pltpu.make_async_remote_copy(src_ref=p_ref.at[2 * px + py], dst_ref=o_ref.at[j], send_sem=ssem.at[j],
                                             recv_sem=rsem.at[j], device_id=(px, py, c), device_id_type=MESH)
                for j, (px, py) in enumerate(chips)]

    def start(si, so, sems):
        for cp in copies(si[0], so[0], *sems):
            cp.start()

    def wait(si, so, sems):
        cps = copies(si[0], so[0], *sems)
        for cp in cps:
            cp.wait_recv()
        for cp in cps:
            cp.wait_send()

    return [p], [_sds((3, rh, w), p.dtype)], [pltpu.SemaphoreType.DMA((3,)), pltpu.SemaphoreType.DMA((3,))], start, wait


def _run_side(name, side):
    s_ins, s_outs, s_sems, s_start, s_wait = side
    n_in, n_out = len(s_ins), len(s_outs)

    def body(*refs):
        si, so, sems = refs[:n_in], refs[n_in:n_in + n_out], refs[n_in + n_out:]
        s_start(si, so, sems)
        s_wait(si, so, sems)

    return pl.pallas_call(body, name=name, in_specs=[ANY] * n_in, out_specs=[ANY] * n_out, out_shape=list(s_outs),
                          scratch_shapes=list(s_sems))(*s_ins)


def _add_chips(p, got, idx_arr):
    _, rh, w = p.shape
    br = _tile(rh, 512, 16)
    nb = rh // br

    def body(idx_ref, p_ref, r_ref, o_ref):
        o_ref[...] = ((p_ref[0].astype(F32) + r_ref[0].astype(F32)) + r_ref[1].astype(F32)) + r_ref[2].astype(F32)

    return pl.pallas_call(
        body, name="add_chips",
        grid_spec=pltpu.PrefetchScalarGridSpec(
            num_scalar_prefetch=1, grid=(nb,),
            in_specs=[pl.BlockSpec((1, br, w), lambda i, idx: (idx[0], i, 0)),
                      pl.BlockSpec((3, br, w), lambda i, idx: (0, i, 0))],
            out_specs=pl.BlockSpec((br, w), lambda i, idx: (idx[1] * nb + i, 0))),
        out_shape=_sds((2 * rh, w)), compiler_params=_params(1),
    )(idx_arr, p, got)


def _join_halves(buf):
    r, w = buf.shape
    rh = r // 2

    def body(b_ref, o_ref, ssem, rsem):
        x, y, c = _place()
        mine = o_ref.at[pl.ds(c * rh, rh), :]
        cp = pltpu.make_async_remote_copy(src_ref=mine, dst_ref=mine, send_sem=ssem, recv_sem=rsem,
                                          device_id=(x, y, 1 - c), device_id_type=MESH)
        cp.start()
        other = o_ref.at[pl.ds((1 - c) * rh, rh), :]
        pltpu.make_async_remote_copy(src_ref=other, dst_ref=other, send_sem=ssem, recv_sem=rsem,
                                     device_id=(x, y, 1 - c), device_id_type=MESH).wait_recv()
        cp.wait_send()

    return pl.pallas_call(
        body, name="join_halves", in_specs=[ANY], out_specs=ANY, out_shape=_sds((r, w), buf.dtype),
        input_output_aliases={0: 0},
        scratch_shapes=[pltpu.SemaphoreType.DMA(()), pltpu.SemaphoreType.DMA(())],
    )(buf)


def _gather_side(flat):
    r, w = flat.shape
    rh = r // 2

    def ici(x_ref, o_ref, ssem, rsem):
        x, y, c = _place()
        chips = [(1 - x, y), (x, 1 - y), (1 - x, 1 - y)]
        mine = pl.ds(c * rh, rh)
        return [pltpu.make_async_remote_copy(src_ref=x_ref.at[mine, :], dst_ref=o_ref.at[2 * x + y, mine, :], send_sem=ssem.at[j],
                                             recv_sem=rsem.at[j], device_id=(px, py, c), device_id_type=MESH)
                for j, (px, py) in enumerate(chips)]

    def start(si, so, sems):
        for cp in ici(si[0], so[0], *sems):
            cp.start()

    def wait(si, so, sems):
        x_ref, o_ref = si[0], so[0]
        ssem, rsem = sems
        x, y, c = _place()
        chips = [(1 - x, y), (x, 1 - y), (1 - x, 1 - y)]
        sib = (x, y, 1 - c)
        mine = pl.ds(c * rh, rh)
        other = pl.ds((1 - c) * rh, rh)
        sends = ici(x_ref, o_ref, ssem, rsem)
        for j, (px, py) in enumerate(chips):
            got = o_ref.at[2 * px + py, mine, :]
            pltpu.make_async_remote_copy(src_ref=x_ref.at[mine, :], dst_ref=got, send_sem=ssem.at[j], recv_sem=rsem.at[j],
                                         device_id=(px, py, c), device_id_type=MESH).wait_recv()
            cp = pltpu.make_async_remote_copy(src_ref=got, dst_ref=got, send_sem=ssem.at[3 + j], recv_sem=rsem.at[3 + j],
                                              device_id=sib, device_id_type=MESH)
            cp.start()
            sends.append(cp)
        for j, (px, py) in enumerate(chips):
            theirs = o_ref.at[2 * px + py, other, :]
            pltpu.make_async_remote_copy(src_ref=theirs, dst_ref=theirs, send_sem=ssem.at[3 + j], recv_sem=rsem.at[3 + j],
                                         device_id=sib, device_id_type=MESH).wait_recv()
        for cp in sends:
            cp.wait_send()

    return ([flat], [_sds((4, r, w), flat.dtype)], [pltpu.SemaphoreType.DMA((6,)), pltpu.SemaphoreType.DMA((6,))],
            start, wait)


BIG = ("w_in", "w_conv_proj", "w_gdn_proj", "w_out", "w_ffn_in", "w_ffn_out")
COL_SHARDED = ("w_in", "w_ffn_in")
SMALL = ("b_in", "conv_dw_w", "conv_dw_b", "conv_ln_g", "conv_ln_b", "b_conv_proj", "short_conv_w", "a_log", "dt_bias",
         "gdn_norm_w", "ln1_g", "ln1_b", "ln2_g", "ln2_b")
SMALL_SHARDED = ("conv_dw_w", "short_conv_w")
ORDER = ("w_in", "b_in", "conv_dw_w", "conv_dw_b", "conv_ln_g", "conv_ln_b", "w_conv_proj", "b_conv_proj", "short_conv_w",
         "a_log", "dt_bias", "gdn_norm_w", "w_gdn_proj", "w_out", "ln1_g", "ln1_b", "w_ffn_in", "w_ffn_out", "ln2_g", "ln2_b")


def _flat_rows(shapes, fw):
    rows = [int(np.prod(sh)) // fw for sh in shapes]
    total = sum(rows)
    pad = (-total) % (2048 if total >= 4096 else 32)
    return rows, total + pad, pad


def _row(v):
    return v.reshape(1, -1)


def _layer_fwd(x, xb, wl, alpha, side=None):
    d = x.shape[1]
    z = _mm("mm_z", xb, wl["w_main"], "nn", bias=wl["b_main"])
    ba = _mm("mm_ba", xb, wl["w_ba"], "nn", bias=wl["b_ba"])
    c3, c1 = _conv_fwd(z, wl["conv_dw_w"], wl["conv_dw_b"], wl["conv_ln_g"], wl["conv_ln_b"], d)
    yc = _mm("mm_proj", c3, wl["w_conv_proj"], "nn", bias=wl["b_conv_proj"])
    qkv = _sconv_fwd(z, wl["short_conv_w"], d, 2)
    s = x.shape[0]
    nh = d // HEAD
    bat = jnp.transpose(ba[:, :2 * nh].reshape(s // CHUNK, CHUNK, 2 * nh), (0, 2, 1))
    o, states, *side_out = _gdn_fwd(qkv, ba, bat, wl["hp"], side)
    og = _gdn_post_fwd(o, z, wl["gdn_norm_w"], d, 5)
    yg = _mm("mm_proj", og, wl["w_gdn_proj"], "nn", bias=wl["zero_bias"])
    m = _merge_fwd(z, yc, yg, d, 6, 7)
    mix = _mm("mm_proj", m, wl["w_out"], "nn", bias=wl["zero_bias"])
    x1, x1b = _ln_res_fwd(x, mix, wl["ln1_g"], wl["ln1_b"], alpha)
    hg = _mm("mm_ffn_in", x1b, wl["w_ffn_g"], "nn", out_dtype=BF16)
    hu = _mm("mm_ffn_in", x1b, wl["w_ffn_u"], "nn", out_dtype=BF16)
    f = _swiglu_fwd(hg, hu)
    ffn = _mm("mm_ffn_out", f, wl["w_ffn_out"], "nn")
    x2, x2b = _ln_res_fwd(x1, ffn, wl["ln2_g"], wl["ln2_b"], alpha)
    saved = dict(x=x, xb=xb, x1b=x1b, z=z, ba=ba, bat=bat, c1=c1, c3=c3, yc=yc, qkv=qkv, o=o, states=states, og=og, yg=yg, m=m, mix=mix,
                 x1=x1, hg=hg, hu=hu, f=f, ffn=ffn)
    return x2, x2b, saved, side_out


def _layer_bwd(dy, sv, wl, alpha, pend=None, c_arr=None):
    x, z, x1, xb, x1b = sv["x"], sv["z"], sv["x1"], sv["xb"], sv["x1b"]
    s, d = x.shape
    nh = d // HEAD
    g = {}
    dr2, dr2b, g["ln2_g"], g["ln2_b"] = _ln_res_bwd(x1, sv["ffn"], wl["ln2_g"], dy, alpha)
    df = _mm("mm_dffn_out", dr2b, wl["w_ffn_out"], "nt")
    g["w_ffn_out"] = _mm("mm_gw_ffn_out", sv["f"], dr2b, "tn")
    dhg, dhu, *swapped = _swiglu_bwd(sv["hg"], sv["hu"], df, None if pend is None else _swap_side(pend))
    pair = None if pend is None else _add_half(pend, swapped[0], c_arr)
    t = _mm("mm_dffn_in", dhg, wl["w_ffn_g"], "nt", res=dr2, res_scale=alpha)
    dx1 = _mm("mm_dffn_in", dhu, wl["w_ffn_u"], "nt", res=t)
    g["w_ffn_in"] = jnp.concatenate([_mm("mm_gw_ffn_in", x1b, dhg, "tn"), _mm("mm_gw_ffn_in", x1b, dhu, "tn")], axis=1)
    dr1, dr1b, g["ln1_g"], g["ln1_b"] = _ln_res_bwd(x, sv["mix"], wl["ln1_g"], dx1, alpha)
    dm = _mm("mm_dproj", dr1b, wl["w_out"], "nt")
    g["w_out"] = _mm("mm_gw_proj", sv["m"], dr1b, "tn")
    dyc, dyg, dz, g["b_conv_proj"], db_gates = _merge_bwd(dm, z, sv["yc"], sv["yg"], d, 6)
    dc3 = _mm("mm_dproj", dyc, wl["w_conv_proj"], "nt")
    g["w_conv_proj"] = _mm("mm_gw_proj", sv["c3"], dyc, "tn")
    dog = _mm("mm_dproj", dyg, wl["w_gdn_proj"], "nt")
    g["w_gdn_proj"] = _mm("mm_gw_proj", sv["og"], dyg, "tn")
    dz, g["conv_dw_w"], g["conv_dw_b"], g["conv_ln_g"], g["conv_ln_b"], db_glu, *got = _conv_bwd(
        z, sv["c1"], dc3, wl["conv_dw_w"], wl["conv_ln_g"], wl["conv_ln_b"], d, dz, None if pair is None else _scatter_side(pair))
    do, dz, g["gdn_norm_w"], db_zgate = _gdn_post_bwd(dog, sv["o"], z, wl["gdn_norm_w"], d, 5, dz)
    dqkv, dbat, dhp = _gdn_bwd(sv["qkv"], sv["ba"], sv["bat"], wl["hp"], sv["states"], do)
    g["a_log"] = dhp[:, 0, 0]
    g["dt_bias"] = dhp[:, 1, 0]
    dz, g["short_conv_w"], db_qkv = _sconv_bwd(z, dqkv, wl["short_conv_w"], d, 2, dz)
    dba = jnp.transpose(dbat[:, :, :2, :], (1, 3, 2, 0)).reshape(s, 2 * nh)
    dba = jnp.pad(dba, ((0, 0), (0, LANE - 2 * nh)))
    t = _mm("mm_dba", dba, wl["w_ba"], "nt", res=dr1, res_scale=alpha)
    dx = _mm("mm_dz", dz, wl["w_main"], "nt", res=t)
    gw_main = _mm("mm_gw_main", xb, dz, "tn")
    gw_ba = _mm("mm_gw_ba", xb, dba, "tn")
    g["w_in"] = jnp.concatenate([gw_main[:, :6 * d], gw_ba[:, :2 * nh], gw_main[:, 6 * d:]], axis=1)
    dbb = _col_sums("colsum_dba", dba)
    g["b_in"] = jnp.concatenate([db_glu, db_qkv, db_zgate, dbb[:, :2 * nh], db_gates], axis=1)
    return dx, g, (None if pend is None else (pair, got[0]))


def kernel(x, w_in, b_in, conv_dw_w, conv_dw_b, conv_ln_g, conv_ln_b, w_conv_proj, b_conv_proj, short_conv_w, a_log, dt_bias, gdn_norm_w, w_gdn_proj, w_out, ln1_g, ln1_b, w_ffn_in, w_ffn_out, ln2_g, ln2_b, loss_target, m_w_in, m_b_in, m_conv_dw_w, m_conv_dw_b, m_conv_ln_g, m_conv_ln_b, m_w_conv_proj, m_b_conv_proj, m_short_conv_w, m_a_log, m_dt_bias, m_gdn_norm_w, m_w_gdn_proj, m_w_out, m_ln1_g, m_ln1_b, m_w_ffn_in, m_w_ffn_out, m_ln2_g, m_ln2_b, v_w_in, v_b_in, v_conv_dw_w, v_conv_dw_b, v_conv_ln_g, v_conv_ln_b, v_w_conv_proj, v_b_conv_proj, v_short_conv_w, v_a_log, v_dt_bias, v_gdn_norm_w, v_w_gdn_proj, v_w_out, v_ln1_g, v_ln1_b, v_w_ffn_in, v_w_ffn_out, v_ln2_g, v_ln2_b):
    wts = dict(w_in=w_in, b_in=b_in, conv_dw_w=conv_dw_w, conv_dw_b=conv_dw_b, conv_ln_g=conv_ln_g, conv_ln_b=conv_ln_b,
               w_conv_proj=w_conv_proj, b_conv_proj=b_conv_proj, short_conv_w=short_conv_w, a_log=a_log, dt_bias=dt_bias,
               gdn_norm_w=gdn_norm_w, w_gdn_proj=w_gdn_proj, w_out=w_out, ln1_g=ln1_g, ln1_b=ln1_b, w_ffn_in=w_ffn_in,
               w_ffn_out=w_ffn_out, ln2_g=ln2_g, ln2_b=ln2_b)
    mom = dict(w_in=m_w_in, b_in=m_b_in, conv_dw_w=m_conv_dw_w, conv_dw_b=m_conv_dw_b, conv_ln_g=m_conv_ln_g,
               conv_ln_b=m_conv_ln_b, w_conv_proj=m_w_conv_proj, b_conv_proj=m_b_conv_proj, short_conv_w=m_short_conv_w,
               a_log=m_a_log, dt_bias=m_dt_bias, gdn_norm_w=m_gdn_norm_w, w_gdn_proj=m_w_gdn_proj, w_out=m_w_out,
               ln1_g=m_ln1_g, ln1_b=m_ln1_b, w_ffn_in=m_w_ffn_in, w_ffn_out=m_w_ffn_out, ln2_g=m_ln2_g, ln2_b=m_ln2_b)
    var = dict(w_in=v_w_in, b_in=v_b_in, conv_dw_w=v_conv_dw_w, conv_dw_b=v_conv_dw_b, conv_ln_g=v_conv_ln_g,
               conv_ln_b=v_conv_ln_b, w_conv_proj=v_w_conv_proj, b_conv_proj=v_b_conv_proj, short_conv_w=v_short_conv_w,
               a_log=v_a_log, dt_bias=v_dt_bias, gdn_norm_w=v_gdn_norm_w, w_gdn_proj=v_w_gdn_proj, w_out=v_w_out,
               ln1_g=v_ln1_g, ln1_b=v_ln1_b, w_ffn_in=v_w_ffn_in, w_ffn_out=v_w_ffn_out, ln2_g=v_ln2_g, ln2_b=v_ln2_b)

    depth = w_in.shape[0]
    _, s, d = x.shape
    nh = d // HEAD
    alpha = float((2.0 * depth) ** 0.25)
    xi, yi, ci = _place()
    chip = 2 * xi + yi
    c_arr = jnp.reshape(ci, (1,)).astype(jnp.int32)
    idx_arr = jnp.stack([chip, ci]).astype(jnp.int32)

    shard_shapes = [wts[n].shape[1:] for n in BIG]
    fw = FLAT_W if all(int(np.prod(sh)) % FLAT_W == 0 for sh in shard_shapes) else LANE
    rows, total_rows, pad_rows = _flat_rows(shard_shapes, fw)

    kw, cs = conv_dw_w.shape[1], conv_dw_w.shape[2]
    ks, ss = short_conv_w.shape[1], short_conv_w.shape[2]
    small_w = jnp.concatenate([conv_dw_w.reshape(depth * kw, cs), jnp.zeros(((-depth * kw) % 8, cs), F32)], axis=0)
    cw_all = _gather_chips(small_w)[:, :depth * kw].reshape(4, depth, kw, cs)
    cw_all = jnp.transpose(cw_all, (1, 2, 0, 3)).reshape(depth, kw, 4 * cs)
    small_s = jnp.concatenate([short_conv_w.reshape(depth * ks, ss), jnp.zeros(((-depth * ks) % 8, ss), F32)], axis=0)
    sw_all = _gather_chips(small_s)[:, :depth * ks].reshape(4, depth, ks, ss)
    sw_all = jnp.transpose(sw_all, (1, 2, 0, 3)).reshape(depth, ks, 4 * ss)

    def my_flat(l):
        parts = [wts[n][l].reshape(-1, fw) for n in BIG]
        if pad_rows:
            parts.append(jnp.zeros((pad_rows, fw), F32))
        return jnp.concatenate(parts, axis=0).astype(BF16)

    def layer_weights(l, others):
        gathered = lax.dynamic_update_slice(others, my_flat(l)[None], (chip, 0, 0))
        full, off = {}, 0
        for n, sh, nr in zip(BIG, shard_shapes, rows):
            blk = gathered[:, off:off + nr, :].reshape((4,) + tuple(sh))
            off += nr
            if n in COL_SHARDED:
                full[n] = jnp.transpose(blk, (1, 0, 2)).reshape(sh[0], 4 * sh[1])
            else:
                full[n] = blk.reshape(4 * sh[0], sh[1])
        wi = full["w_in"]
        bi = b_in[l]
        f_ff = full["w_ffn_in"].shape[1] // 2
        return dict(
            conv_dw_w=cw_all[l], short_conv_w=sw_all[l],
            w_main=jnp.concatenate([wi[:, :6 * d], wi[:, 6 * d + 2 * nh:]], axis=1),
            w_ba=jnp.pad(wi[:, 6 * d:6 * d + 2 * nh], ((0, 0), (0, LANE - 2 * nh))),
            b_main=_row(jnp.concatenate([bi[:6 * d], bi[6 * d + 2 * nh:]])),
            b_ba=_row(jnp.pad(bi[6 * d:6 * d + 2 * nh], (0, LANE - 2 * nh))),
            w_conv_proj=full["w_conv_proj"], w_gdn_proj=full["w_gdn_proj"], w_out=full["w_out"],
            w_ffn_g=full["w_ffn_in"][:, :f_ff], w_ffn_u=full["w_ffn_in"][:, f_ff:], w_ffn_out=full["w_ffn_out"],
            b_conv_proj=_row(b_conv_proj[l]), zero_bias=jnp.zeros((1, d), F32),
            conv_dw_b=_row(conv_dw_b[l]), conv_ln_g=_row(conv_ln_g[l]), conv_ln_b=_row(conv_ln_b[l]),
            gdn_norm_w=_row(gdn_norm_w[l]), ln1_g=_row(ln1_g[l]), ln1_b=_row(ln1_b[l]), ln2_g=_row(ln2_g[l]), ln2_b=_row(ln2_b[l]),
            hp=jnp.concatenate([jnp.broadcast_to(a_log[l][:, None, None], (nh, 1, LANE)),
                                jnp.broadcast_to(dt_bias[l][:, None, None], (nh, 1, LANE)),
                                jnp.zeros((nh, 6, LANE), F32)], axis=1),
        )

    h = x[0]
    hb16 = h.astype(BF16)
    saved, layers = [], []
    others = _run_side("gather_split", _gather_side(my_flat(0)))[0]
    for l in range(depth):
        layers.append(layer_weights(l, others))
        side = _gather_side(my_flat(l + 1)) if l + 1 < depth else None
        h, hb16, sv, side_out = _layer_fwd(h, hb16, layers[l], alpha, side)
        if side is not None:
            others = side_out[0]
        saved.append(sv)
    loss_blk, dy = _loss_fwd_bwd(h, loss_target[0])
    loss = lax.psum(0.5 * loss_blk[0, 0], ("x", "y", "c"))

    grads = [None] * depth
    shard_grads = [None] * depth

    def finish(l, pair, got):
        shard = _join_halves(_add_chips(pair, got, idx_arr))
        sg, off = {}, 0
        for n, sh, nr in zip(BIG, shard_shapes, rows):
            sg[n] = shard[off:off + nr].reshape(sh)
            off += nr
        shard_grads[l] = sg

    pending = None
    for l in reversed(range(depth)):
        dy, g, reduced = _layer_bwd(dy, saved[l], layers[l], alpha, None if pending is None else pending[1], c_arr)
        if pending is not None:
            finish(pending[0], *reduced)
        grads[l] = g
        parts = []
        for n, sh in zip(BIG, shard_shapes):
            gf = g[n]
            if n in COL_SHARDED:
                gs = jnp.transpose(gf.reshape(sh[0], 4, sh[1]), (1, 0, 2))
            else:
                gs = gf.reshape(4, sh[0], sh[1])
            parts.append(gs.reshape(4, -1, fw))
        if pad_rows:
            parts.append(jnp.zeros((4, pad_rows, fw), F32))
        gflat = jnp.concatenate(parts, axis=1)
        pending = (l, gflat)
    pair = _add_half(pending[1], _run_side("swap_halves", _swap_side(pending[1]))[0], c_arr)
    finish(pending[0], pair, _run_side("scatter_chips", _scatter_side(pair))[0])
    grad_x = dy[None]

    small_parts = [jnp.concatenate([grads[l][n].reshape(-1) for l in range(depth)]) for n in SMALL]
    sizes = [int(p.shape[0]) for p in small_parts]
    flat = jnp.concatenate(small_parts)
    nflat = int(flat.shape[0])
    flat = jnp.pad(flat, (0, (-nflat) % (8 * LANE))).reshape(-1, LANE)
    tot = _sum8(_gather_all(flat)).reshape(-1)
    small_g, off = {}, 0
    for n, sz in zip(SMALL, sizes):
        full = tot[off:off + sz]
        off += sz
        if n == "conv_dw_w":
            full = lax.dynamic_slice_in_dim(full.reshape(depth, kw, 4 * cs), chip * cs, cs, axis=2)
        elif n == "short_conv_w":
            full = lax.dynamic_slice_in_dim(full.reshape(depth, ks, 4 * ss), chip * ss, ss, axis=2)
        small_g[n] = full.reshape(wts[n].shape)

    out_g, out_d, out_m, out_v = {}, {}, {}, {}
    for n in BIG:
        gfull = jnp.stack([shard_grads[l][n] for l in range(depth)])
        sh = wts[n].shape
        two = (sh[0] * sh[1], sh[2])
        dl, nm, nv = _adamw("adamw_" + n, wts[n].reshape(two), gfull.reshape(two), mom[n].reshape(two), var[n].reshape(two))
        out_g[n], out_d[n], out_m[n], out_v[n] = gfull, dl.reshape(sh), nm.reshape(sh), nv.reshape(sh)

    def pack(src):
        v = jnp.concatenate([src[n].reshape(-1) for n in SMALL])
        return jnp.pad(v, (0, (-int(v.shape[0])) % (8 * LANE))).reshape(-1, LANE)

    dl, nm, nv = _adamw("adamw_small", pack(wts), pack(small_g), pack(mom), pack(var))
    off = 0
    for n in SMALL:
        sz = int(np.prod(wts[n].shape))
        out_g[n] = small_g[n]
        out_d[n] = dl.reshape(-1)[off:off + sz].reshape(wts[n].shape)
        out_m[n] = nm.reshape(-1)[off:off + sz].reshape(wts[n].shape)
        out_v[n] = nv.reshape(-1)[off:off + sz].reshape(wts[n].shape)
        off += sz

    return (loss, grad_x, *[out_g[n] for n in ORDER], *[out_d[n] for n in ORDER], *[out_m[n] for n in ORDER],
            *[out_v[n] for n in ORDER])
```

```python
import functools

import jax
import jax.numpy as jnp
import numpy as np
from jax import lax
from jax.experimental import pallas as pl
from jax.experimental.pallas import tpu as pltpu

F32 = jnp.float32
BF16 = jnp.bfloat16
MESH = pl.DeviceIdType.MESH

LN_EPS = 1e-5
L2_EPS = 1e-6
CHUNK = 64
HEAD = 128
CONV_HALO = 32
SHORT_HALO = 8
ROW_BLOCK = 256
CONV_ROWS = 64
VMEM_LIMIT = 56 * 1024 * 1024
LANE = 128
FLAT_W = 1024

ADAM_LR, ADAM_B1, ADAM_B2, ADAM_EPS, ADAM_WD, ADAM_STEP = 0.001, 0.9, 0.999, 1e-08, 0.01, 10

NN = ((1,), (0,))
NT = ((1,), (1,))
TN = ((0,), (0,))


def _params(n_axes):
    return pltpu.CompilerParams(dimension_semantics=("arbitrary",) * n_axes, vmem_limit_bytes=VMEM_LIMIT)


def _tile(dim, pref, unit=LANE):
    if dim <= pref:
        return dim
    best = None
    for t in range(unit, pref + 1, unit):
        if dim % t == 0:
            best = t
    assert best is not None, (dim, pref, unit)
    return best


def _dotb(a, b, dims):
    return lax.dot_general(a.astype(BF16), b.astype(BF16), (dims, ((), ())), preferred_element_type=F32)


def _split(a):
    hi = a.astype(BF16)
    return hi, (a - hi.astype(F32)).astype(BF16)


def _sig(x):
    return jax.nn.sigmoid(x)


def _dsilu(x):
    s = _sig(x)
    return s * (1.0 + x * (1.0 - s))


def _softplus(x):
    return jnp.maximum(x, 0.0) + jnp.log(1.0 + jnp.exp(-jnp.abs(x)))


def _iota(shape, dim):
    return lax.broadcasted_iota(jnp.int32, shape, dim)


def _accum(ref, val, first):
    @pl.when(first)
    def _():
        ref[...] = val

    @pl.when(jnp.logical_not(first))
    def _():
        ref[...] += val


def _mm(name, a, b, mode, out_dtype=F32, bias=None, res=None, res_scale=1.0, tm=1024, tn=1024, tk=2048):
    if mode == "nn":
        (m, k), (k2, n) = a.shape, b.shape
    elif mode == "tn":
        (k, m), (k2, n) = a.shape, b.shape
    else:
        (m, k), (n, k2) = a.shape, b.shape
    assert k == k2, (name, a.shape, b.shape)
    tm, tn, tk = _tile(m, tm), _tile(n, tn), _tile(k, tk)
    nk = k // tk
    dims = {"nn": NN, "tn": TN, "nt": NT}[mode]
    a_spec = {"nn": pl.BlockSpec((tm, tk), lambda i, j, kk: (i, kk)),
              "tn": pl.BlockSpec((tk, tm), lambda i, j, kk: (kk, i)),
              "nt": pl.BlockSpec((tm, tk), lambda i, j, kk: (i, kk))}[mode]
    b_spec = {"nn": pl.BlockSpec((tk, tn), lambda i, j, kk: (kk, j)),
              "tn": pl.BlockSpec((tk, tn), lambda i, j, kk: (kk, j)),
              "nt": pl.BlockSpec((tn, tk), lambda i, j, kk: (j, kk))}[mode]
    ins, in_specs = [a, b], [a_spec, b_spec]
    if bias is not None:
        ins.append(bias)
        in_specs.append(pl.BlockSpec((1, tn), lambda i, j, kk: (0, j)))
    if res is not None:
        ins.append(res)
        in_specs.append(pl.BlockSpec((tm, tn), lambda i, j, kk: (i, j)))
    has_bias, has_res = bias is not None, res is not None

    def body(*refs):
        a_ref, b_ref = refs[0], refs[1]
        pos = 2
        bias_ref = res_ref = None
        if has_bias:
            bias_ref = refs[pos]
            pos += 1
        if has_res:
            res_ref = refs[pos]
            pos += 1
        o_ref = refs[pos]
        part = _dotb(a_ref[...], b_ref[...], dims)

        def finish(r):
            if has_bias:
                r = r + bias_ref[...]
            if has_res:
                r = r + res_scale * res_ref[...]
            o_ref[...] = r.astype(out_dtype)

        if nk == 1:
            finish(part)
        else:
            acc_ref = refs[pos + 1]
            kk = pl.program_id(2)
            _accum(acc_ref, part, kk == 0)

            @pl.when(kk == nk - 1)
            def _():
                finish(acc_ref[...])

    return pl.pallas_call(
        body, name=name, grid=(m // tm, n // tn, nk), in_specs=in_specs,
        out_specs=pl.BlockSpec((tm, tn), lambda i, j, kk: (i, j)),
        out_shape=jax.ShapeDtypeStruct((m, n), out_dtype),
        scratch_shapes=[pltpu.VMEM((tm, tn), F32)] if nk > 1 else [], compiler_params=_params(3),
    )(*ins)


def _ew(name, fn, grid, ins, in_specs, out_shapes, out_specs, scratch=(), side=None, fill=None):
    s_ins, s_outs, s_sems, s_start, s_wait = side if side is not None else ((), (), (), None, None)
    n_in, n_out, n_si, n_so, n_scr = len(ins), len(out_shapes), len(s_ins), len(s_outs), len(scratch)
    n_fill = 0 if fill is None else 1

    def body(*refs):
        pos = [0]

        def take(n):
            part = refs[pos[0]:pos[0] + n]
            pos[0] += n
            return part

        i_refs, _filled, si_refs = take(n_in), take(n_fill), take(n_si)
        o_refs, so_refs, scr, sems = take(n_out), take(n_so), take(n_scr), take(len(s_sems))
        if side is not None:
            ids = [pl.program_id(a) for a in range(len(grid))]
            first = functools.reduce(jnp.logical_and, [p == 0 for p in ids])
            last = functools.reduce(jnp.logical_and, [p == g - 1 for p, g in zip(ids, grid)])

            @pl.when(first)
            def _():
                s_start(si_refs, so_refs, sems)

        fn(i_refs, o_refs, scr)
        if side is not None:
            @pl.when(last)
            def _():
                s_wait(si_refs, so_refs, sems)

    out = pl.pallas_call(
        body, name=name, grid=grid, in_specs=list(in_specs) + [ANY] * (n_fill + n_si),
        out_specs=list(out_specs) + [ANY] * n_so, out_shape=list(out_shapes) + list(s_outs),
        scratch_shapes=list(scratch) + list(s_sems), compiler_params=_params(len(grid)),
        input_output_aliases={} if fill is None else {n_in: fill[1]},
    )(*ins, *([] if fill is None else [fill[0]]), *s_ins)
    return out


def _sds(shape, dtype=F32):
    return jax.ShapeDtypeStruct(tuple(shape), dtype)


def _rows(bs, w, col=0):
    return pl.BlockSpec((bs, w), lambda i, col=col: (i, col))


def _par(r, w):
    return pl.BlockSpec((r, w), lambda i: (0, 0))


def _ln_stats(r):
    mu = jnp.mean(r, axis=-1, keepdims=True)
    xc = r - mu
    var = jnp.mean(xc * xc, axis=-1, keepdims=True)
    rstd = lax.rsqrt(var + LN_EPS)
    return xc * rstd, rstd


def _ln_back(xh, rstd, g, dy):
    dxh = dy * g
    m1 = jnp.mean(dxh, axis=-1, keepdims=True)
    m2 = jnp.mean(dxh * xh, axis=-1, keepdims=True)
    return rstd * (dxh - m1 - xh * m2)


def _colsum(v):
    return jnp.sum(v, axis=0, keepdims=True)


def _ln_res_fwd(x, sub, g, b, alpha):
    s, d = x.shape
    bs = min(ROW_BLOCK, s)

    def fn(i_refs, o_refs, _):
        x_ref, s_ref, g_ref, b_ref = i_refs
        xh, _r = _ln_stats(alpha * x_ref[...] + s_ref[...])
        y = xh * g_ref[...] + b_ref[...]
        o_refs[0][...] = y
        o_refs[1][...] = y.astype(BF16)

    return _ew("ln_res_fwd", fn, (s // bs,), [x, sub, g, b], [_rows(bs, d), _rows(bs, d), _par(1, d), _par(1, d)],
               [_sds((s, d)), _sds((s, d), BF16)], [_rows(bs, d), _rows(bs, d)])


def _ln_res_bwd(x, sub, g, dy, alpha):
    s, d = x.shape
    bs = min(ROW_BLOCK, s)

    def fn(i_refs, o_refs, _):
        x_ref, s_ref, g_ref, dy_ref = i_refs
        dr_ref, drb_ref, dg_ref, db_ref = o_refs
        first = pl.program_id(0) == 0
        xh, rstd = _ln_stats(alpha * x_ref[...] + s_ref[...])
        dy_v = dy_ref[...]
        dr = _ln_back(xh, rstd, g_ref[...], dy_v)
        dr_ref[...] = dr
        drb_ref[...] = dr.astype(BF16)
        _accum(dg_ref, _colsum(dy_v * xh), first)
        _accum(db_ref, _colsum(dy_v), first)

    return _ew("ln_res_bwd", fn, (s // bs,), [x, sub, g, dy], [_rows(bs, d), _rows(bs, d), _par(1, d), _rows(bs, d)],
               [_sds((s, d)), _sds((s, d), BF16), _sds((1, d)), _sds((1, d))],
               [_rows(bs, d), _rows(bs, d), _par(1, d), _par(1, d)])


def _loss_fwd_bwd(y, t):
    s, d = y.shape
    bs = min(ROW_BLOCK, s)

    def fn(i_refs, o_refs, _):
        err = i_refs[0][...] - i_refs[1][...]
        o_refs[1][...] = err * (1.0 / d)
        tot = jnp.sum(jnp.sum(err * err, axis=1, keepdims=True), axis=0, keepdims=True) * (1.0 / d)
        _accum(o_refs[0], jnp.broadcast_to(tot, (8, LANE)), pl.program_id(0) == 0)

    return _ew("loss", fn, (s // bs,), [y, t], [_rows(bs, d), _rows(bs, d)],
               [_sds((8, LANE)), _sds((s, d))], [_par(8, LANE), _rows(bs, d)])


def _conv_fwd(z, cw, cb, g, b, d):
    s = z.shape[0]
    kw = cw.shape[0]
    bs = min(ROW_BLOCK, s)
    hb = CONV_HALO
    r = bs // hb

    def prev(col):
        return pl.BlockSpec((hb, d), lambda i: (jnp.maximum(i * r - 1, 0), col))

    def fn(i_refs, o_refs, scr):
        a_ref, b_ref, ap_ref, bp_ref, w_ref, cb_ref, g_ref, be_ref = i_refs
        c3_ref, c1_ref = o_refs
        ext = scr[0]
        i = pl.program_id(0)
        ext[pl.ds(0, hb), :] = jnp.where(i > 0, ap_ref[...] * _sig(bp_ref[...]), 0.0)
        ext[pl.ds(hb, bs), :] = a_ref[...] * _sig(b_ref[...])
        for st in range(d // LANE):
            sl = pl.ds(st * LANE, LANE)
            for rb in range(bs // CONV_ROWS):
                acc = jnp.zeros((CONV_ROWS, LANE), F32) + cb_ref[:, sl]
                for j in range(kw):
                    acc = acc + w_ref[pl.ds(j, 1), sl] * ext[pl.ds(rb * CONV_ROWS + hb - (kw - 1) + j, CONV_ROWS), sl]
                c1_ref[pl.ds(rb * CONV_ROWS, CONV_ROWS), sl] = acc
        xh, _r = _ln_stats(c1_ref[...])
        c2 = xh * g_ref[...] + be_ref[...]
        c3_ref[...] = (c2 * _sig(c2)).astype(BF16)

    return _ew("conv_fwd", fn, (s // bs,), [z, z, z, z, cw, cb, g, b],
               [_rows(bs, d, 0), _rows(bs, d, 1), prev(0), prev(1), _par(kw, d), _par(1, d), _par(1, d), _par(1, d)],
               [_sds((s, d), BF16), _sds((s, d))], [_rows(bs, d), _rows(bs, d)], scratch=[pltpu.VMEM((hb + bs, d), F32)])


def _conv_bwd(z, c1, dc3, cw, g, b, d, dz, side=None):
    s = z.shape[0]
    kw = cw.shape[0]
    bs = min(ROW_BLOCK, s)
    hb = CONV_HALO
    r = bs // hb
    nrow = s // bs
    last_h = s // hb - 1

    def prev(col):
        return pl.BlockSpec((hb, d), lambda i: (jnp.maximum(i * r - 1, 0), col))

    def nxt(col):
        return pl.BlockSpec((hb, d), lambda i: (jnp.minimum((i + 1) * r, last_h), col))

    def fn(i_refs, o_refs, scr):
        a_ref, b_ref, ap_ref, bp_ref, c_ref, cn_ref, d_ref, dn_ref, w_ref, g_ref, be_ref = i_refs
        dz_ref, dw_ref, dcb_ref, dg_ref, dbe_ref, dbz_ref = o_refs
        ext, extd = scr
        i = pl.program_id(0)
        first = i == 0
        more = i < nrow - 1
        ext[pl.ds(0, hb), :] = jnp.where(i > 0, ap_ref[...] * _sig(bp_ref[...]), 0.0)
        ext[pl.ds(hb, bs), :] = a_ref[...] * _sig(b_ref[...])
        c1_e = jnp.concatenate([c_ref[...], jnp.where(more, cn_ref[...], 0.0)], axis=0)
        xh, rstd = _ln_stats(c1_e)
        c2 = xh * g_ref[...] + be_ref[...]
        dc3_e = jnp.concatenate([d_ref[...], jnp.where(more, dn_ref[...], 0.0)], axis=0)
        dc2 = dc3_e * _dsilu(c2)
        dc1 = _ln_back(xh, rstd, g_ref[...], dc2)
        extd[...] = dc1
        _accum(dg_ref, _colsum((dc2 * xh)[:bs]), first)
        _accum(dbe_ref, _colsum(dc2[:bs]), first)
        _accum(dcb_ref, _colsum(dc1[:bs]), first)

        @pl.when(first)
        def _():
            dw_ref[...] = jnp.zeros_like(dw_ref)
            dbz_ref[...] = jnp.zeros_like(dbz_ref)

        for st in range(d // LANE):
            sl = pl.ds(st * LANE, LANE)
            sl_gate = pl.ds(d + st * LANE, LANE)
            for rb in range(bs // CONV_ROWS):
                r0 = rb * CONV_ROWS
                own = extd[pl.ds(r0, CONV_ROWS), sl]
                dc0 = jnp.zeros((CONV_ROWS, LANE), F32)
                for j in range(kw):
                    dc0 = dc0 + w_ref[pl.ds(j, 1), sl] * extd[pl.ds(r0 + kw - 1 - j, CONV_ROWS), sl]
                    dw_ref[pl.ds(j, 1), sl] += _colsum(own * ext[pl.ds(r0 + hb - (kw - 1) + j, CONV_ROWS), sl])
                rows = pl.ds(r0, CONV_ROWS)
                a_v, sb = a_ref[rows, sl], _sig(b_ref[rows, sl])
                da = dc0 * sb
                db = dc0 * a_v * sb * (1.0 - sb)
                dz_ref[rows, sl] = da.astype(BF16)
                dz_ref[rows, sl_gate] = db.astype(BF16)
                dbz_ref[:, sl] += _colsum(da)
                dbz_ref[:, sl_gate] += _colsum(db)

    return _ew("conv_bwd", fn, (nrow,), [z, z, z, z, c1, c1, dc3, dc3, cw, g, b],
               [_rows(bs, d, 0), _rows(bs, d, 1), prev(0), prev(1), _rows(bs, d), nxt(0), _rows(bs, d), nxt(0),
                _par(kw, d), _par(1, d), _par(1, d)],
               [_sds(dz.shape, BF16), _sds((kw, d)), _sds((1, d)), _sds((1, d)), _sds((1, d)), _sds((1, 2 * d))],
               [_rows(bs, 2 * d, 0), _par(kw, d), _par(1, d), _par(1, d), _par(1, d), _par(1, 2 * d)],
               scratch=[pltpu.VMEM((bs + hb, d), F32), pltpu.VMEM((bs + hb, d), F32)], side=side, fill=(dz, 0))


def _sconv_fwd(z, sw, d, col0):
    s = z.shape[0]
    kw = sw.shape[0]
    bs = min(ROW_BLOCK, s)
    hb = SHORT_HALO
    r = bs // hb

    def fn(i_refs, o_refs, scr):
        x_ref, xp_ref, w_ref = i_refs
        ext = scr[0]
        i = pl.program_id(1)
        ext[pl.ds(0, hb), :] = jnp.where(i > 0, xp_ref[...], 0.0)
        ext[pl.ds(hb, bs), :] = x_ref[...]
        acc = jnp.zeros((bs, d), F32)
        for j in range(kw):
            acc = acc + w_ref[pl.ds(j, 1), :] * ext[pl.ds(hb - (kw - 1) + j, bs), :]
        o_refs[0][0] = acc * _sig(acc)

    return _ew("sconv_fwd", fn, (3, s // bs), [z, z, sw],
               [pl.BlockSpec((bs, d), lambda sg, i: (i, col0 + sg)),
                pl.BlockSpec((hb, d), lambda sg, i: (jnp.maximum(i * r - 1, 0), col0 + sg)),
                pl.BlockSpec((kw, d), lambda sg, i: (0, sg))],
               [_sds((3, s, d))], [pl.BlockSpec((1, bs, d), lambda sg, i: (sg, i, 0))],
               scratch=[pltpu.VMEM((hb + bs, d), F32)])[0]


def _sconv_bwd(z, dy, sw, d, col0, dz):
    s = z.shape[0]
    kw = sw.shape[0]
    bs = min(ROW_BLOCK, s)
    hb = SHORT_HALO
    r = bs // hb
    nrow = s // bs
    last_h = s // hb - 1

    def fn(i_refs, o_refs, scr):
        x_ref, xp_ref, xn_ref, d_ref, dn_ref, w_ref = i_refs
        dx_ref, dw_ref, dbz_ref = o_refs
        ext, extd = scr
        i = pl.program_id(1)
        more = i < nrow - 1
        ext[pl.ds(0, hb), :] = jnp.where(i > 0, xp_ref[...], 0.0)
        ext[pl.ds(hb, bs), :] = x_ref[...]
        ext[pl.ds(hb + bs, hb), :] = jnp.where(more, xn_ref[...], 0.0)
        n1 = bs + hb
        pre = jnp.zeros((n1, d), F32)
        for j in range(kw):
            pre = pre + w_ref[pl.ds(j, 1), :] * ext[pl.ds(hb - (kw - 1) + j, n1), :]
        dy_e = jnp.concatenate([d_ref[0], jnp.where(more, dn_ref[0], 0.0)], axis=0)
        extd[...] = dy_e * _dsilu(pre)

        @pl.when(i == 0)
        def _():
            dw_ref[...] = jnp.zeros_like(dw_ref)

        dp_own = extd[pl.ds(0, bs), :]
        dx = jnp.zeros((bs, d), F32)
        for j in range(kw):
            dx = dx + w_ref[pl.ds(j, 1), :] * extd[pl.ds(kw - 1 - j, bs), :]
            dw_ref[pl.ds(j, 1), :] += _colsum(dp_own * ext[pl.ds(hb - (kw - 1) + j, bs), :])
        dx_ref[...] = dx.astype(BF16)
        _accum(dbz_ref, _colsum(dx), i == 0)

    return _ew("sconv_bwd", fn, (3, nrow), [z, z, z, dy, dy, sw],
               [pl.BlockSpec((bs, d), lambda sg, i: (i, col0 + sg)),
                pl.BlockSpec((hb, d), lambda sg, i: (jnp.maximum(i * r - 1, 0), col0 + sg)),
                pl.BlockSpec((hb, d), lambda sg, i: (jnp.minimum((i + 1) * r, last_h), col0 + sg)),
                pl.BlockSpec((1, bs, d), lambda sg, i: (sg, i, 0)),
                pl.BlockSpec((1, hb, d), lambda sg, i: (sg, jnp.minimum((i + 1) * r, last_h), 0)),
                pl.BlockSpec((kw, d), lambda sg, i: (0, sg))],
               [_sds(dz.shape, BF16), _sds((kw, 3 * d)), _sds((1, 3 * d))],
               [pl.BlockSpec((bs, d), lambda sg, i: (i, col0 + sg)), pl.BlockSpec((kw, d), lambda sg, i: (0, sg)),
                pl.BlockSpec((1, d), lambda sg, i: (0, sg))],
               scratch=[pltpu.VMEM((bs + 2 * hb, d), F32), pltpu.VMEM((bs + hb, d), F32)], fill=(dz, 0))


GDN_HEADS = 16

BNN = (((2,), (1,)), ((0,), (0,)))
BNT = (((2,), (2,)), ((0,), (0,)))
BTN = (((1,), (1,)), ((0,), (0,)))


def _bdot(a, b, dn):
    return lax.dot_general(a.astype(BF16), b.astype(BF16), dn, preferred_element_type=F32)


def _bdotf(a, b, dn):
    ah, al = _split(a)
    bh, bl = _split(b)

    def dot(u, v):
        return lax.dot_general(u, v, dn, preferred_element_type=F32)

    return dot(ah, bh) + (dot(ah, bl) + dot(al, bh))


def _gdn_pre(qkv_ref, ba_ref, bat_ref, hp_ref, hb, tinv=None):
    g0 = pl.program_id(0) * hb
    c = CHUNK

    def heads(part):
        return jnp.stack([qkv_ref[part, :, pl.ds(hh * HEAD, HEAD)] for hh in range(hb)], axis=0)

    qr, kr, v = heads(0), heads(1), heads(2)
    ba = ba_ref[...]
    bat = bat_ref[0]
    nh = bat.shape[0] // 2
    lane = _iota(ba.shape, 1)
    sub = _iota(bat.shape, 0)

    def col(off):
        return jnp.stack([jnp.sum(jnp.where(lane == g0 + hh + off, ba, 0.0), axis=1, keepdims=True) for hh in range(hb)], axis=0)

    def row(off):
        return jnp.stack([jnp.sum(jnp.where(sub == g0 + hh + off, bat, 0.0), axis=0, keepdims=True) for hh in range(hb)], axis=0)

    braw_c, araw_c, braw_r, araw_r = col(0), col(nh), row(0), row(nh)
    alog = jnp.max(hp_ref[:, pl.ds(0, 1), :], axis=2, keepdims=True)
    dtb = jnp.max(hp_ref[:, pl.ds(1, 1), :], axis=2, keepdims=True)
    nega = -jnp.exp(alog)
    beta_c, beta_r = _sig(braw_c), _sig(braw_r)
    la_c = nega * _softplus(araw_c + dtb)
    la_r = nega * _softplus(araw_r + dtb)
    i = _iota((c, c), 0)
    j = _iota((c, c), 1)
    g_c = jnp.sum(jnp.where(j <= i, la_r, 0.0), axis=2, keepdims=True)
    g_r = jnp.sum(jnp.where(i <= j, la_c, 0.0), axis=1, keepdims=True)
    g_last = jnp.sum(la_c, axis=1, keepdims=True)
    low = i >= j
    dec = jnp.where(low, jnp.exp(jnp.where(low, g_c - g_r, 0.0)), 0.0)
    rq = lax.rsqrt(jnp.sum(qr * qr, axis=2, keepdims=True) + L2_EPS)
    rk = lax.rsqrt(jnp.sum(kr * kr, axis=2, keepdims=True) + L2_EPS)
    q = qr * (rq * HEAD ** -0.5)
    k = kr * rk
    kb = k * beta_c
    lmat = jnp.where(i > j, _bdot(kb, k, BNT) * dec, 0.0)
    eye = jnp.where(i == j, 1.0, 0.0)
    if tinv is None:
        tinv = eye - lmat
        pw = lmat
        for _ in range(int(np.log2(c)) - 1):
            pw = _bdotf(pw, pw, BNN)
            tinv = _bdotf(tinv, eye + pw, BNN)
    eg_c = jnp.exp(g_c)
    rhs_w = kb * eg_c
    sol = _bdotf(tinv, jnp.concatenate([v * beta_c, rhs_w], axis=2), BNN)
    attn = jnp.where(low, _bdot(q, k, BNT) * dec, 0.0)
    ekd = jnp.exp(g_last - g_c)
    return dict(qr=qr, kr=kr, v=v, rq=rq, rk=rk, q=q, k=k, kb=kb, beta_c=beta_c, beta_r=beta_r, la_r=la_r,
                araw_r=araw_r, dtb=dtb, nega=nega, g_c=g_c, g_last=g_last, dec=dec, lmat=lmat, tinv=tinv,
                eg_c=eg_c, rhs_w=rhs_w, sol=sol, u=sol[:, :, :HEAD], w=sol[:, :, HEAD:], attn=attn, q_dec=q * eg_c,
                ekd=ekd, k_dec=k * ekd, i=i, j=j, low=low)


def _gdn_fwd(qkv, ba, bat, hp, side=None):
    _, s, d = qkv.shape
    nh, n, c = d // HEAD, s // CHUNK, CHUNK
    hb = min(GDN_HEADS, nh)
    gw = hb * HEAD

    def fn(i_refs, o_refs, scr):
        o_ref, st_ref, ti_ref = o_refs
        s_scr = scr[0]
        st = jnp.where(pl.program_id(1) == 0, 0.0, s_scr[...])
        p = _gdn_pre(*i_refs, hb)
        vn = p["u"] - _bdot(p["w"], st, BNN)
        o = _bdot(p["q_dec"], st, BNN) + _bdot(p["attn"], vn, BNN)
        s_scr[...] = st * jnp.exp(p["g_last"]) + _bdot(p["k_dec"], vn, BTN)
        st_ref[:, 0] = st
        ti_ref[:, 0] = p["tinv"]
        for hh in range(hb):
            o_ref[:, pl.ds(hh * HEAD, HEAD)] = o[hh]

    return _ew("gdn_fwd", fn, (nh // hb, n), [qkv, ba, bat, hp],
               [pl.BlockSpec((3, c, gw), lambda h, t: (0, t, h)), pl.BlockSpec((c, LANE), lambda h, t: (t, 0)),
                pl.BlockSpec((1, bat.shape[1], c), lambda h, t: (t, 0, 0)), pl.BlockSpec((hb, 8, LANE), lambda h, t: (h, 0, 0))],
               [_sds((s, d)), _sds((nh, n, HEAD, HEAD)), _sds((nh, n, c, c))],
               [pl.BlockSpec((c, gw), lambda h, t: (t, h)), pl.BlockSpec((hb, 1, HEAD, HEAD), lambda h, t: (h, t, 0, 0)),
                pl.BlockSpec((hb, 1, c, c), lambda h, t: (h, t, 0, 0))],
               scratch=[pltpu.VMEM((hb, HEAD, HEAD), F32)], side=side)


def _gdn_bwd(qkv, ba, bat, hp, states, tinvs, do):
    _, s, d = qkv.shape
    nh, n, c = d // HEAD, s // CHUNK, CHUNK
    hb = min(GDN_HEADS, nh)
    gw = hb * HEAD

    def fn(i_refs, o_refs, scr):
        qkv_ref, ba_ref, bat_ref, hp_ref, st_ref, ti_ref, do_ref = i_refs
        dqkv_ref, dbat_ref, dhp_ref = o_refs
        ds_scr = scr[0]
        first = pl.program_id(1) == 0
        ds1 = jnp.where(first, 0.0, ds_scr[...])
        dhp_old = jnp.where(first, 0.0, dhp_ref[...])
        st = st_ref[:, 0]
        do_v = jnp.stack([do_ref[:, pl.ds(hh * HEAD, HEAD)] for hh in range(hb)], axis=0)
        p = _gdn_pre(qkv_ref, ba_ref, bat_ref, hp_ref, hb, ti_ref[:, 0])
        i, j, low = p["i"], p["j"], p["low"]
        u, w, sol, attn, dec = p["u"], p["w"], p["sol"], p["attn"], p["dec"]
        q, k, kb, v = p["q"], p["k"], p["kb"], p["v"]
        eg_c, ekd, beta_c = p["eg_c"], p["ekd"], p["beta_c"]
        egl = jnp.exp(p["g_last"])

        def rsum(a):
            return jnp.sum(a, axis=2, keepdims=True)

        def csum(a):
            return jnp.sum(a, axis=1, keepdims=True)

        vn = u - _bdot(w, st, BNN)
        dvn = _bdot(attn, do_v, BTN) + _bdot(p["k_dec"], ds1, BNN)
        dattn = jnp.where(low, _bdot(do_v, vn, BNT), 0.0)
        dqd = _bdot(do_v, st, BNT)
        dkd = _bdot(vn, ds1, BNT)
        ds_scr[...] = _bdot(p["q_dec"], do_v, BTN) + egl * ds1 - _bdot(w, dvn, BTN)
        dgl = csum(rsum(st * ds1)) * egl
        dw = -_bdot(dvn, st, BNT)
        drhs = _bdotf(p["tinv"], jnp.concatenate([dvn, dw], axis=2), BTN)
        da = -jnp.where(i > j, _bdot(drhs, sol, BNT), 0.0)
        mm = da * p["lmat"] + dattn * attn
        dg_c = rsum(mm)
        dg_r = -csum(mm)
        dkk = da * dec
        dqk = dattn * dec
        dkb = _bdot(dkk, k, BNN)
        dk = _bdot(dkk, kb, BTN) + _bdot(dqk, q, BTN)
        dq = _bdot(dqk, k, BNN) + dqd * eg_c
        dg_c = dg_c + rsum(dqd * p["q_dec"])
        dk = dk + dkd * ekd
        t = rsum(dkd * p["k_dec"])
        dgl = dgl + csum(t)
        dg_c = dg_c - t
        drhs_u, drhs_w = drhs[:, :, :HEAD], drhs[:, :, HEAD:]
        dkb = dkb + drhs_w * eg_c
        dg_c = dg_c + rsum(drhs_w * p["rhs_w"])
        dv_out = drhs_u * beta_c
        dbeta_c = rsum(drhs_u * v) + rsum(dkb * k)
        dk = dk + dkb * beta_c
        diag = i == j
        dg = dg_c + rsum(jnp.where(diag, dg_r, 0.0))
        dla_r = csum(jnp.where(low, dg, 0.0)) + dgl
        dbeta_r = csum(jnp.where(diag, dbeta_c, 0.0))
        beta_r = p["beta_r"]
        dbraw_r = dbeta_r * beta_r * (1.0 - beta_r)
        daraw_r = dla_r * p["nega"] * _sig(p["araw_r"] + p["dtb"])
        dalog = rsum(dla_r * p["la_r"])
        ddtb = rsum(daraw_r)
        row8 = _iota((8, c), 0)
        dbat_ref[:, 0] = jnp.where(row8 == 0, dbraw_r, jnp.where(row8 == 1, daraw_r, 0.0))
        rowp = _iota((8, LANE), 0)
        dhp_ref[...] = dhp_old + jnp.where(rowp == 0, dalog, jnp.where(rowp == 1, ddtb, 0.0))
        qr, kr, rq, rk = p["qr"], p["kr"], p["rq"], p["rk"]
        sc = HEAD ** -0.5
        dq_out = sc * (rq * dq - qr * (rq * rq * rq * rsum(dq * qr)))
        dk_out = rk * dk - kr * (rk * rk * rk * rsum(dk * kr))
        for hh in range(hb):
            sl = pl.ds(hh * HEAD, HEAD)
            dqkv_ref[0, :, sl] = dq_out[hh]
            dqkv_ref[1, :, sl] = dk_out[hh]
            dqkv_ref[2, :, sl] = dv_out[hh]

    rev = n - 1
    return _ew("gdn_bwd", fn, (nh // hb, n), [qkv, ba, bat, hp, states, tinvs, do],
               [pl.BlockSpec((3, c, gw), lambda h, t: (0, rev - t, h)), pl.BlockSpec((c, LANE), lambda h, t: (rev - t, 0)),
                pl.BlockSpec((1, bat.shape[1], c), lambda h, t: (rev - t, 0, 0)), pl.BlockSpec((hb, 8, LANE), lambda h, t: (h, 0, 0)),
                pl.BlockSpec((hb, 1, HEAD, HEAD), lambda h, t: (h, rev - t, 0, 0)),
                pl.BlockSpec((hb, 1, c, c), lambda h, t: (h, rev - t, 0, 0)), pl.BlockSpec((c, gw), lambda h, t: (rev - t, h))],
               [_sds((3, s, d)), _sds((nh, n, 8, c)), _sds((nh, 8, LANE))],
               [pl.BlockSpec((3, c, gw), lambda h, t: (0, rev - t, h)), pl.BlockSpec((hb, 1, 8, c), lambda h, t: (h, rev - t, 0, 0)),
                pl.BlockSpec((hb, 8, LANE), lambda h, t: (h, 0, 0))],
               scratch=[pltpu.VMEM((hb, HEAD, HEAD), F32)])


def _gdn_post_fwd(o, z, gw, d, zcol):
    s = o.shape[0]
    bs = min(ROW_BLOCK, s)

    def fn(i_refs, o_refs, _):
        o_ref, z_ref, w_ref = i_refs
        wv = w_ref[...]
        for h in range(d // HEAD):
            sl = pl.ds(h * HEAD, HEAD)
            oh, zg = o_ref[:, sl], z_ref[:, sl]
            r = lax.rsqrt(jnp.mean(oh * oh, axis=1, keepdims=True) + L2_EPS)
            o_refs[0][:, sl] = (oh * r * wv * (zg * _sig(zg))).astype(BF16)

    return _ew("gdn_post_fwd", fn, (s // bs,), [o, z, gw], [_rows(bs, d), _rows(bs, d, zcol), _par(1, HEAD)],
               [_sds((s, d), BF16)], [_rows(bs, d)])[0]


def _gdn_post_bwd(dog, o, z, gw, d, zcol, dz):
    s = o.shape[0]
    bs = min(ROW_BLOCK, s)

    def fn(i_refs, o_refs, _):
        g_ref, o_ref, z_ref, w_ref = i_refs
        do_ref, dz_ref, dw_ref, dbz_ref = o_refs
        first = pl.program_id(0) == 0
        wv = w_ref[...]
        dwacc = jnp.zeros((1, HEAD), F32)
        dbz = []
        for h in range(d // HEAD):
            sl = pl.ds(h * HEAD, HEAD)
            oh, zg, gv = o_ref[:, sl], z_ref[:, sl], g_ref[:, sl]
            r = lax.rsqrt(jnp.mean(oh * oh, axis=1, keepdims=True) + L2_EPS)
            on = oh * r
            sil = zg * _sig(zg)
            dzg = gv * on * wv * _dsilu(zg)
            dz_ref[:, sl] = dzg.astype(BF16)
            dbz.append(_colsum(dzg))
            dwacc = dwacc + _colsum(gv * on * sil)
            don = gv * wv * sil
            do_ref[:, sl] = r * (don - on * jnp.mean(don * on, axis=1, keepdims=True))
        _accum(dw_ref, dwacc, first)
        _accum(dbz_ref, jnp.concatenate(dbz, axis=1), first)

    return _ew("gdn_post_bwd", fn, (s // bs,), [dog, o, z, gw],
               [_rows(bs, d), _rows(bs, d), _rows(bs, d, zcol), _par(1, HEAD)],
               [_sds((s, d)), _sds(dz.shape, BF16), _sds((1, HEAD)), _sds((1, d))],
               [_rows(bs, d), _rows(bs, d, zcol), _par(1, HEAD), _par(1, d)], fill=(dz, 1))


def _merge_fwd(z, yc, yg, d, col_a, col_b):
    s = z.shape[0]
    bs = min(ROW_BLOCK, s)

    def fn(i_refs, o_refs, _):
        ga, gb, yc_ref, yg_ref = i_refs
        o_refs[0][...] = (_sig(ga[...]) * yc_ref[...] + _sig(gb[...]) * yg_ref[...]).astype(BF16)

    return _ew("merge_fwd", fn, (s // bs,), [z, z, yc, yg],
               [_rows(bs, d, col_a), _rows(bs, d, col_b), _rows(bs, d), _rows(bs, d)], [_sds((s, d), BF16)], [_rows(bs, d)])[0]


def _merge_bwd(dm, z, yc, yg, d, col_a):
    s, zw = z.shape
    bs = min(ROW_BLOCK, s)

    def fn(i_refs, o_refs, _):
        dm_ref, ga, gb, yc_ref, yg_ref = i_refs
        dyc_ref, dyg_ref, dz_ref, dbc_ref, dbz_ref = o_refs
        first = pl.program_id(0) == 0
        dmv = dm_ref[...]
        sa, sb = _sig(ga[...]), _sig(gb[...])
        dyc = dmv * sa
        dyc_ref[...] = dyc.astype(BF16)
        dyg_ref[...] = (dmv * sb).astype(BF16)
        dga = dmv * yc_ref[...] * sa * (1.0 - sa)
        dgb = dmv * yg_ref[...] * sb * (1.0 - sb)
        dz_ref[:, pl.ds(0, d)] = dga.astype(BF16)
        dz_ref[:, pl.ds(d, d)] = dgb.astype(BF16)
        _accum(dbc_ref, _colsum(dyc), first)
        _accum(dbz_ref, jnp.concatenate([_colsum(dga), _colsum(dgb)], axis=1), first)

    return _ew("merge_bwd", fn, (s // bs,), [dm, z, z, yc, yg],
               [_rows(bs, d), _rows(bs, d, col_a), _rows(bs, d, col_a + 1), _rows(bs, d), _rows(bs, d)],
               [_sds((s, d), BF16), _sds((s, d), BF16), _sds((s, zw), BF16), _sds((1, d)), _sds((1, 2 * d))],
               [_rows(bs, d), _rows(bs, d), _rows(bs, 2 * d, col_a // 2), _par(1, d), _par(1, 2 * d)])


def _swiglu_fwd(hg, hu):
    s, f = hg.shape
    bs = min(ROW_BLOCK, s)
    cw = _tile(f, 2048)

    def fn(i_refs, o_refs, _):
        g = i_refs[0][...].astype(F32)
        o_refs[0][...] = (g * _sig(g) * i_refs[1][...].astype(F32)).astype(BF16)

    spec = pl.BlockSpec((bs, cw), lambda i, j: (i, j))
    return _ew("swiglu_fwd", fn, (s // bs, f // cw), [hg, hu], [spec, spec], [_sds((s, f), BF16)], [spec])[0]


def _swiglu_bwd(hg, hu, df, side=None):
    s, f = hg.shape
    bs = min(ROW_BLOCK, s)
    cw = _tile(f, 2048)

    def fn(i_refs, o_refs, _):
        g, u, dfv = i_refs[0][...].astype(F32), i_refs[1][...].astype(F32), i_refs[2][...]
        o_refs[0][...] = (dfv * u * _dsilu(g)).astype(BF16)
        o_refs[1][...] = (dfv * g * _sig(g)).astype(BF16)

    spec = pl.BlockSpec((bs, cw), lambda i, j: (i, j))
    return _ew("swiglu_bwd", fn, (s // bs, f // cw), [hg, hu, df], [spec] * 3, [_sds((s, f), BF16)] * 2, [spec] * 2,
               side=side)


def _col_sums(name, a):
    s, n = a.shape
    bs = min(ROW_BLOCK, s)
    cw = _tile(n, 2048)

    def fn(i_refs, o_refs, _):
        _accum(o_refs[0], _colsum(i_refs[0][...].astype(F32)), pl.program_id(1) == 0)

    return _ew(name, fn, (n // cw, s // bs), [a], [pl.BlockSpec((bs, cw), lambda j, i: (i, j))],
               [_sds((1, n))], [pl.BlockSpec((1, cw), lambda j, i: (0, j))])[0]


def _adamw(name, w, g, m, v):
    r, c = w.shape
    br = r
    if r * c * 4 > (1 << 20):
        cands = [t for t in range(8, r, 8) if r % t == 0 and t * c * 4 <= (1 << 20)]
        br = max(cands) if cands else r
    c1 = 1.0 - ADAM_B1 ** ADAM_STEP
    c2 = 1.0 - ADAM_B2 ** ADAM_STEP

    def fn(i_refs, o_refs, _):
        wv, gv, mv, vv = (x[...] for x in i_refs)
        m2 = ADAM_B1 * mv + (1.0 - ADAM_B1) * gv
        v2 = ADAM_B2 * vv + (1.0 - ADAM_B2) * (gv * gv)
        o_refs[0][...] = -ADAM_LR * ((m2 / c1) / (jnp.sqrt(v2 / c2) + ADAM_EPS) + ADAM_WD * wv)
        o_refs[1][...] = m2
        o_refs[2][...] = v2

    spec = pl.BlockSpec((br, c), lambda i: (i, 0))
    return _ew(name, fn, (r // br,), [w, g, m, v], [spec] * 4, [_sds((r, c))] * 3, [spec] * 3)


ANY = pl.BlockSpec(memory_space=pl.ANY)


def _place():
    return lax.axis_index("x"), lax.axis_index("y"), lax.axis_index("c")


def _gather_chips(flat):
    r, w = flat.shape

    def body(x_ref, o_ref, ssem, rsem, lsem):
        x, y, c = _place()
        me = 2 * x + y
        chips = [(1 - x, y), (x, 1 - y), (1 - x, 1 - y)]
        local = pltpu.make_async_copy(x_ref, o_ref.at[me], lsem)
        local.start()
        sends = []
        for j, (px, py) in enumerate(chips):
            cp = pltpu.make_async_remote_copy(src_ref=x_ref, dst_ref=o_ref.at[me], send_sem=ssem.at[j], recv_sem=rsem.at[j],
                                              device_id=(px, py, c), device_id_type=MESH)
            cp.start()
            sends.append(cp)
        for j, (px, py) in enumerate(chips):
            pltpu.make_async_remote_copy(src_ref=x_ref, dst_ref=o_ref.at[2 * px + py], send_sem=ssem.at[j], recv_sem=rsem.at[j],
                                         device_id=(px, py, c), device_id_type=MESH).wait_recv()
        for cp in sends:
            cp.wait_send()
        local.wait()

    return pl.pallas_call(
        body, name="gather_chips", in_specs=[ANY], out_specs=ANY, out_shape=_sds((4, r, w), flat.dtype),
        scratch_shapes=[pltpu.SemaphoreType.DMA((3,)), pltpu.SemaphoreType.DMA((3,)), pltpu.SemaphoreType.DMA(())],
    )(flat)


def _gather_all(flat):
    r, w = flat.shape
    masks = [(mx, my, mc) for mx in (0, 1) for my in (0, 1) for mc in (0, 1)][1:]

    def body(x_ref, o_ref, ssem, rsem, lsem):
        x, y, c = _place()
        me = 4 * x + 2 * y + c
        local = pltpu.make_async_copy(x_ref, o_ref.at[me], lsem)
        local.start()
        peers = [(x ^ mx, y ^ my, c ^ mc) for (mx, my, mc) in masks]
        sends = []
        for j, peer in enumerate(peers):
            cp = pltpu.make_async_remote_copy(src_ref=x_ref, dst_ref=o_ref.at[me], send_sem=ssem.at[j], recv_sem=rsem.at[j],
                                              device_id=peer, device_id_type=MESH)
            cp.start()
            sends.append(cp)
        for j, (px, py, pc) in enumerate(peers):
            pltpu.make_async_remote_copy(src_ref=x_ref, dst_ref=o_ref.at[4 * px + 2 * py + pc], send_sem=ssem.at[j],
                                         recv_sem=rsem.at[j], device_id=(px, py, pc), device_id_type=MESH).wait_recv()
        for cp in sends:
            cp.wait_send()
        local.wait()

    return pl.pallas_call(
        body, name="gather_all", in_specs=[ANY], out_specs=ANY, out_shape=_sds((8, r, w), flat.dtype),
        scratch_shapes=[pltpu.SemaphoreType.DMA((7,)), pltpu.SemaphoreType.DMA((7,)), pltpu.SemaphoreType.DMA(())],
    )(flat)


def _sum8(parts):
    _, r, w = parts.shape

    def fn(i_refs, o_refs, _):
        acc = i_refs[0][0]
        for k in range(1, 8):
            acc = acc + i_refs[0][k]
        o_refs[0][...] = acc

    br = _tile(r, 512, 8)
    return _ew("sum8", fn, (r // br,), [parts], [pl.BlockSpec((8, br, w), lambda i: (0, i, 0))], [_sds((r, w))],
               [pl.BlockSpec((br, w), lambda i: (i, 0))])[0]


def _swap_side(g):
    _, r, w = g.shape
    rh = r // 2

    def copy(g_ref, o_ref, ssem, rsem):
        x, y, c = _place()
        return pltpu.make_async_remote_copy(src_ref=g_ref.at[:, pl.ds((1 - c) * rh, rh), :], dst_ref=o_ref, send_sem=ssem,
                                            recv_sem=rsem, device_id=(x, y, 1 - c), device_id_type=MESH)

    def start(si, so, sems):
        copy(si[0], so[0], *sems).start()

    def wait(si, so, sems):
        copy(si[0], so[0], *sems).wait()

    return [g], [_sds((4, rh, w), g.dtype)], [pltpu.SemaphoreType.DMA(()), pltpu.SemaphoreType.DMA(())], start, wait


def _add_half(g, got, c_arr):
    _, r, w = g.shape
    rh = r // 2
    br = _tile(rh, 512, 16)
    nb = rh // br

    def body(c_ref, g_ref, r_ref, o_ref):
        o_ref[...] = (g_ref[...] + r_ref[...]).astype(BF16)

    return pl.pallas_call(
        body, name="add_half",
        grid_spec=pltpu.PrefetchScalarGridSpec(
            num_scalar_prefetch=1, grid=(4, nb),
            in_specs=[pl.BlockSpec((1, br, w), lambda s, i, c_ref: (s, c_ref[0] * nb + i, 0)),
                      pl.BlockSpec((1, br, w), lambda s, i, c_ref: (s, i, 0))],
            out_specs=pl.BlockSpec((1, br, w), lambda s, i, c_ref: (s, i, 0))),
        out_shape=_sds((4, rh, w), BF16), compiler_params=_params(2),
    )(c_arr, g, got)


def _scatter_side(p):
    _, rh, w = p.shape

    def copies(p_ref, o_ref, ssem, rsem):
        x, y, c = _place()
        chips = [(1 - x, y), (x, 1 - y), (1 - x, 1 - y)]
        return [pltpu.make_async_remote_copy(src_ref=p_ref.at[2 * px + py], dst_ref=o_ref.at[j], send_sem=ssem.at[j],
                                             recv_sem=rsem.at[j], device_id=(px, py, c), device_id_type=MESH)
                for j, (px, py) in enumerate(chips)]

    def start(si, so, sems):
        for cp in copies(si[0], so[0], *sems):
            cp.start()

    def wait(si, so, sems):
        cps = copies(si[0], so[0], *sems)
        for cp in cps:
            cp.wait_recv()
        for cp in cps:
            cp.wait_send()

    return [p], [_sds((3, rh, w), p.dtype)], [pltpu.SemaphoreType.DMA((3,)), pltpu.SemaphoreType.DMA((3,))], start, wait


def _run_side(name, side):
    s_ins, s_outs, s_sems, s_start, s_wait = side
    n_in, n_out = len(s_ins), len(s_outs)

    def body(*refs):
        si, so, sems = refs[:n_in], refs[n_in:n_in + n_out], refs[n_in + n_out:]
        s_start(si, so, sems)
        s_wait(si, so, sems)

    return pl.pallas_call(body, name=name, in_specs=[ANY] * n_in, out_specs=[ANY] * n_out, out_shape=list(s_outs),
                          scratch_shapes=list(s_sems))(*s_ins)


def _add_chips(p, got, idx_arr):
    _, rh, w = p.shape
    br = _tile(rh, 512, 16)
    nb = rh // br

    def body(idx_ref, p_ref, r_ref, o_ref):
        o_ref[...] = ((p_ref[0].astype(F32) + r_ref[0].astype(F32)) + r_ref[1].astype(F32)) + r_ref[2].astype(F32)

    return pl.pallas_call(
        body, name="add_chips",
        grid_spec=pltpu.PrefetchScalarGridSpec(
            num_scalar_prefetch=1, grid=(nb,),
            in_specs=[pl.BlockSpec((1, br, w), lambda i, idx: (idx[0], i, 0)),
                      pl.BlockSpec((3, br, w), lambda i, idx: (0, i, 0))],
            out_specs=pl.BlockSpec((br, w), lambda i, idx: (idx[1] * nb + i, 0))),
        out_shape=_sds((2 * rh, w)), compiler_params=_params(1),
    )(idx_arr, p, got)


def _join_halves(buf):
    r, w = buf.shape
    rh = r // 2

    def body(b_ref, o_ref, ssem, rsem):
        x, y, c = _place()
        mine = o_ref.at[pl.ds(c * rh, rh), :]
        cp = pltpu.make_async_remote_copy(src_ref=mine, dst_ref=mine, send_sem=ssem, recv_sem=rsem,
                                          device_id=(x, y, 1 - c), device_id_type=MESH)
        cp.start()
        other = o_ref.at[pl.ds((1 - c) * rh, rh), :]
        pltpu.make_async_remote_copy(src_ref=other, dst_ref=other, send_sem=ssem, recv_sem=rsem,
                                     device_id=(x, y, 1 - c), device_id_type=MESH).wait_recv()
        cp.wait_send()

    return pl.pallas_call(
        body, name="join_halves", in_specs=[ANY], out_specs=ANY, out_shape=_sds((r, w), buf.dtype),
        input_output_aliases={0: 0},
        scratch_shapes=[pltpu.SemaphoreType.DMA(()), pltpu.SemaphoreType.DMA(())],
    )(buf)


def _gather_side(flat):
    r, w = flat.shape
    rh = r // 2

    def ici(x_ref, o_ref, ssem, rsem):
        x, y, c = _place()
        chips = [(1 - x, y), (x, 1 - y), (1 - x, 1 - y)]
        mine = pl.ds(c * rh, rh)
        return [pltpu.make_async_remote_copy(src_ref=x_ref.at[mine, :], dst_ref=o_ref.at[2 * x + y, mine, :], send_sem=ssem.at[j],
                                             recv_sem=rsem.at[j], device_id=(px, py, c), device_id_type=MESH)
                for j, (px, py) in enumerate(chips)]

    def start(si, so, sems):
        for cp in ici(si[0], so[0], *sems):
            cp.start()

    def wait(si, so, sems):
        x_ref, o_ref = si[0], so[0]
        ssem, rsem = sems
        x, y, c = _place()
        chips = [(1 - x, y), (x, 1 - y), (1 - x, 1 - y)]
        sib = (x, y, 1 - c)
        mine = pl.ds(c * rh, rh)
        other = pl.ds((1 - c) * rh, rh)
        sends = ici(x_ref, o_ref, ssem, rsem)
        for j, (px, py) in enumerate(chips):
            got = o_ref.at[2 * px + py, mine, :]
            pltpu.make_async_remote_copy(src_ref=x_ref.at[mine, :], dst_ref=got, send_sem=ssem.at[j], recv_sem=rsem.at[j],
                                         device_id=(px, py, c), device_id_type=MESH).wait_recv()
            cp = pltpu.make_async_remote_copy(src_ref=got, dst_ref=got, send_sem=ssem.at[3 + j], recv_sem=rsem.at[3 + j],
                                              device_id=sib, device_id_type=MESH)
            cp.start()
            sends.append(cp)
        for j, (px, py) in enumerate(chips):
            theirs = o_ref.at[2 * px + py, other, :]
            pltpu.make_async_remote_copy(src_ref=theirs, dst_ref=theirs, send_sem=ssem.at[3 + j], recv_sem=rsem.at[3 + j],
                                         device_id=sib, device_id_type=MESH).wait_recv()
        for cp in sends:
            cp.wait_send()

    return ([flat], [_sds((4, r, w), flat.dtype)], [pltpu.SemaphoreType.DMA((6,)), pltpu.SemaphoreType.DMA((6,))],
            start, wait)


BIG = ("w_in", "w_conv_proj", "w_gdn_proj", "w_out", "w_ffn_in", "w_ffn_out")
COL_SHARDED = ("w_in", "w_ffn_in")
SMALL = ("b_in", "conv_dw_w", "conv_dw_b", "conv_ln_g", "conv_ln_b", "b_conv_proj", "short_conv_w", "a_log", "dt_bias",
         "gdn_norm_w", "ln1_g", "ln1_b", "ln2_g", "ln2_b")
SMALL_SHARDED = ("conv_dw_w", "short_conv_w")
ORDER = ("w_in", "b_in", "conv_dw_w", "conv_dw_b", "conv_ln_g", "conv_ln_b", "w_conv_proj", "b_conv_proj", "short_conv_w",
         "a_log", "dt_bias", "gdn_norm_w", "w_gdn_proj", "w_out", "ln1_g", "ln1_b", "w_ffn_in", "w_ffn_out", "ln2_g", "ln2_b")


def _flat_rows(shapes, fw):
    rows = [int(np.prod(sh)) // fw for sh in shapes]
    total = sum(rows)
    pad = (-total) % (2048 if total >= 4096 else 32)
    return rows, total + pad, pad


def _row(v):
    return v.reshape(1, -1)


def _layer_fwd(x, xb, wl, alpha, side=None):
    d = x.shape[1]
    z = _mm("mm_z", xb, wl["w_main"], "nn", bias=wl["b_main"])
    ba = _mm("mm_ba", xb, wl["w_ba"], "nn", bias=wl["b_ba"])
    c3, c1 = _conv_fwd(z, wl["conv_dw_w"], wl["conv_dw_b"], wl["conv_ln_g"], wl["conv_ln_b"], d)
    yc = _mm("mm_proj", c3, wl["w_conv_proj"], "nn", bias=wl["b_conv_proj"])
    qkv = _sconv_fwd(z, wl["short_conv_w"], d, 2)
    s = x.shape[0]
    nh = d // HEAD
    bat = jnp.transpose(ba[:, :2 * nh].reshape(s // CHUNK, CHUNK, 2 * nh), (0, 2, 1))
    o, states, tinvs, *side_out = _gdn_fwd(qkv, ba, bat, wl["hp"], side)
    og = _gdn_post_fwd(o, z, wl["gdn_norm_w"], d, 5)
    yg = _mm("mm_proj", og, wl["w_gdn_proj"], "nn", bias=wl["zero_bias"])
    m = _merge_fwd(z, yc, yg, d, 6, 7)
    mix = _mm("mm_proj", m, wl["w_out"], "nn", bias=wl["zero_bias"])
    x1, x1b = _ln_res_fwd(x, mix, wl["ln1_g"], wl["ln1_b"], alpha)
    hg = _mm("mm_ffn_in", x1b, wl["w_ffn_g"], "nn", out_dtype=BF16)
    hu = _mm("mm_ffn_in", x1b, wl["w_ffn_u"], "nn", out_dtype=BF16)
    f = _swiglu_fwd(hg, hu)
    ffn = _mm("mm_ffn_out", f, wl["w_ffn_out"], "nn")
    x2, x2b = _ln_res_fwd(x1, ffn, wl["ln2_g"], wl["ln2_b"], alpha)
    saved = dict(x=x, xb=xb, x1b=x1b, z=z, ba=ba, bat=bat, c1=c1, c3=c3, yc=yc, qkv=qkv, o=o, states=states, tinvs=tinvs, og=og, yg=yg, m=m, mix=mix,
                 x1=x1, hg=hg, hu=hu, f=f, ffn=ffn)
    return x2, x2b, saved, side_out


def _layer_bwd(dy, sv, wl, alpha, pend=None, c_arr=None):
    x, z, x1, xb, x1b = sv["x"], sv["z"], sv["x1"], sv["xb"], sv["x1b"]
    s, d = x.shape
    nh = d // HEAD
    g = {}
    dr2, dr2b, g["ln2_g"], g["ln2_b"] = _ln_res_bwd(x1, sv["ffn"], wl["ln2_g"], dy, alpha)
    df = _mm("mm_dffn_out", dr2b, wl["w_ffn_out"], "nt")
    g["w_ffn_out"] = _mm("mm_gw_ffn_out", sv["f"], dr2b, "tn")
    dhg, dhu, *swapped = _swiglu_bwd(sv["hg"], sv["hu"], df, None if pend is None else _swap_side(pend))
    pair = None if pend is None else _add_half(pend, swapped[0], c_arr)
    t = _mm("mm_dffn_in", dhg, wl["w_ffn_g"], "nt", res=dr2, res_scale=alpha)
    dx1 = _mm("mm_dffn_in", dhu, wl["w_ffn_u"], "nt", res=t)
    g["w_ffn_in"] = jnp.concatenate([_mm("mm_gw_ffn_in", x1b, dhg, "tn"), _mm("mm_gw_ffn_in", x1b, dhu, "tn")], axis=1)
    dr1, dr1b, g["ln1_g"], g["ln1_b"] = _ln_res_bwd(x, sv["mix"], wl["ln1_g"], dx1, alpha)
    dm = _mm("mm_dproj", dr1b, wl["w_out"], "nt")
    g["w_out"] = _mm("mm_gw_proj", sv["m"], dr1b, "tn")
    dyc, dyg, dz, g["b_conv_proj"], db_gates = _merge_bwd(dm, z, sv["yc"], sv["yg"], d, 6)
    dc3 = _mm("mm_dproj", dyc, wl["w_conv_proj"], "nt")
    g["w_conv_proj"] = _mm("mm_gw_proj", sv["c3"], dyc, "tn")
    dog = _mm("mm_dproj", dyg, wl["w_gdn_proj"], "nt")
    g["w_gdn_proj"] = _mm("mm_gw_proj", sv["og"], dyg, "tn")
    dz, g["conv_dw_w"], g["conv_dw_b"], g["conv_ln_g"], g["conv_ln_b"], db_glu, *got = _conv_bwd(
        z, sv["c1"], dc3, wl["conv_dw_w"], wl["conv_ln_g"], wl["conv_ln_b"], d, dz, None if pair is None else _scatter_side(pair))
    do, dz, g["gdn_norm_w"], db_zgate = _gdn_post_bwd(dog, sv["o"], z, wl["gdn_norm_w"], d, 5, dz)
    dqkv, dbat, dhp = _gdn_bwd(sv["qkv"], sv["ba"], sv["bat"], wl["hp"], sv["states"], sv["tinvs"], do)
    g["a_log"] = dhp[:, 0, 0]
    g["dt_bias"] = dhp[:, 1, 0]
    dz, g["short_conv_w"], db_qkv = _sconv_bwd(z, dqkv, wl["short_conv_w"], d, 2, dz)
    dba = jnp.transpose(dbat[:, :, :2, :], (1, 3, 2, 0)).reshape(s, 2 * nh)
    dba = jnp.pad(dba, ((0, 0), (0, LANE - 2 * nh)))
    t = _mm("mm_dba", dba, wl["w_ba"], "nt", res=dr1, res_scale=alpha)
    dx = _mm("mm_dz", dz, wl["w_main"], "nt", res=t)
    gw_main = _mm("mm_gw_main", xb, dz, "tn")
    gw_ba = _mm("mm_gw_ba", xb, dba, "tn")
    g["w_in"] = jnp.concatenate([gw_main[:, :6 * d], gw_ba[:, :2 * nh], gw_main[:, 6 * d:]], axis=1)
    dbb = _col_sums("colsum_dba", dba)
    g["b_in"] = jnp.concatenate([db_glu, db_qkv, db_zgate, dbb[:, :2 * nh], db_gates], axis=1)
    return dx, g, (None if pend is None else (pair, got[0]))


def kernel(x, w_in, b_in, conv_dw_w, conv_dw_b, conv_ln_g, conv_ln_b, w_conv_proj, b_conv_proj, short_conv_w, a_log, dt_bias, gdn_norm_w, w_gdn_proj, w_out, ln1_g, ln1_b, w_ffn_in, w_ffn_out, ln2_g, ln2_b, loss_target, m_w_in, m_b_in, m_conv_dw_w, m_conv_dw_b, m_conv_ln_g, m_conv_ln_b, m_w_conv_proj, m_b_conv_proj, m_short_conv_w, m_a_log, m_dt_bias, m_gdn_norm_w, m_w_gdn_proj, m_w_out, m_ln1_g, m_ln1_b, m_w_ffn_in, m_w_ffn_out, m_ln2_g, m_ln2_b, v_w_in, v_b_in, v_conv_dw_w, v_conv_dw_b, v_conv_ln_g, v_conv_ln_b, v_w_conv_proj, v_b_conv_proj, v_short_conv_w, v_a_log, v_dt_bias, v_gdn_norm_w, v_w_gdn_proj, v_w_out, v_ln1_g, v_ln1_b, v_w_ffn_in, v_w_ffn_out, v_ln2_g, v_ln2_b):
    wts = dict(w_in=w_in, b_in=b_in, conv_dw_w=conv_dw_w, conv_dw_b=conv_dw_b, conv_ln_g=conv_ln_g, conv_ln_b=conv_ln_b,
               w_conv_proj=w_conv_proj, b_conv_proj=b_conv_proj, short_conv_w=short_conv_w, a_log=a_log, dt_bias=dt_bias,
               gdn_norm_w=gdn_norm_w, w_gdn_proj=w_gdn_proj, w_out=w_out, ln1_g=ln1_g, ln1_b=ln1_b, w_ffn_in=w_ffn_in,
               w_ffn_out=w_ffn_out, ln2_g=ln2_g, ln2_b=ln2_b)
    mom = dict(w_in=m_w_in, b_in=m_b_in, conv_dw_w=m_conv_dw_w, conv_dw_b=m_conv_dw_b, conv_ln_g=m_conv_ln_g,
               conv_ln_b=m_conv_ln_b, w_conv_proj=m_w_conv_proj, b_conv_proj=m_b_conv_proj, short_conv_w=m_short_conv_w,
               a_log=m_a_log, dt_bias=m_dt_bias, gdn_norm_w=m_gdn_norm_w, w_gdn_proj=m_w_gdn_proj, w_out=m_w_out,
               ln1_g=m_ln1_g, ln1_b=m_ln1_b, w_ffn_in=m_w_ffn_in, w_ffn_out=m_w_ffn_out, ln2_g=m_ln2_g, ln2_b=m_ln2_b)
    var = dict(w_in=v_w_in, b_in=v_b_in, conv_dw_w=v_conv_dw_w, conv_dw_b=v_conv_dw_b, conv_ln_g=v_conv_ln_g,
               conv_ln_b=v_conv_ln_b, w_conv_proj=v_w_conv_proj, b_conv_proj=v_b_conv_proj, short_conv_w=v_short_conv_w,
               a_log=v_a_log, dt_bias=v_dt_bias, gdn_norm_w=v_gdn_norm_w, w_gdn_proj=v_w_gdn_proj, w_out=v_w_out,
               ln1_g=v_ln1_g, ln1_b=v_ln1_b, w_ffn_in=v_w_ffn_in, w_ffn_out=v_w_ffn_out, ln2_g=v_ln2_g, ln2_b=v_ln2_b)

    depth = w_in.shape[0]
    _, s, d = x.shape
    nh = d // HEAD
    alpha = float((2.0 * depth) ** 0.25)
    xi, yi, ci = _place()
    chip = 2 * xi + yi
    c_arr = jnp.reshape(ci, (1,)).astype(jnp.int32)
    idx_arr = jnp.stack([chip, ci]).astype(jnp.int32)

    shard_shapes = [wts[n].shape[1:] for n in BIG]
    fw = FLAT_W if all(int(np.prod(sh)) % FLAT_W == 0 for sh in shard_shapes) else LANE
    rows, total_rows, pad_rows = _flat_rows(shard_shapes, fw)

    kw, cs = conv_dw_w.shape[1], conv_dw_w.shape[2]
    ks, ss = short_conv_w.shape[1], short_conv_w.shape[2]
    small_w = jnp.concatenate([conv_dw_w.reshape(depth * kw, cs), jnp.zeros(((-depth * kw) % 8, cs), F32)], axis=0)
    cw_all = _gather_chips(small_w)[:, :depth * kw].reshape(4, depth, kw, cs)
    cw_all = jnp.transpose(cw_all, (1, 2, 0, 3)).reshape(depth, kw, 4 * cs)
    small_s = jnp.concatenate([short_conv_w.reshape(depth * ks, ss), jnp.zeros(((-depth * ks) % 8, ss), F32)], axis=0)
    sw_all = _gather_chips(small_s)[:, :depth * ks].reshape(4, depth, ks, ss)
    sw_all = jnp.transpose(sw_all, (1, 2, 0, 3)).reshape(depth, ks, 4 * ss)

    def my_flat(l):
        parts = [wts[n][l].reshape(-1, fw) for n in BIG]
        if pad_rows:
            parts.append(jnp.zeros((pad_rows, fw), F32))
        return jnp.concatenate(parts, axis=0).astype(BF16)

    def layer_weights(l, others):
        gathered = lax.dynamic_update_slice(others, my_flat(l)[None], (chip, 0, 0))
        full, off = {}, 0
        for n, sh, nr in zip(BIG, shard_shapes, rows):
            blk = gathered[:, off:off + nr, :].reshape((4,) + tuple(sh))
            off += nr
            if n in COL_SHARDED:
                full[n] = jnp.transpose(blk, (1, 0, 2)).reshape(sh[0], 4 * sh[1])
            else:
                full[n] = blk.reshape(4 * sh[0], sh[1])
        wi = full["w_in"]
        bi = b_in[l]
        f_ff = full["w_ffn_in"].shape[1] // 2
        return dict(
            conv_dw_w=cw_all[l], short_conv_w=sw_all[l],
            w_main=jnp.concatenate([wi[:, :6 * d], wi[:, 6 * d + 2 * nh:]], axis=1),
            w_ba=jnp.pad(wi[:, 6 * d:6 * d + 2 * nh], ((0, 0), (0, LANE - 2 * nh))),
            b_main=_row(jnp.concatenate([bi[:6 * d], bi[6 * d + 2 * nh:]])),
            b_ba=_row(jnp.pad(bi[6 * d:6 * d + 2 * nh], (0, LANE - 2 * nh))),
            w_conv_proj=full["w_conv_proj"], w_gdn_proj=full["w_gdn_proj"], w_out=full["w_out"],
            w_ffn_g=full["w_ffn_in"][:, :f_ff], w_ffn_u=full["w_ffn_in"][:, f_ff:], w_ffn_out=full["w_ffn_out"],
            b_conv_proj=_row(b_conv_proj[l]), zero_bias=jnp.zeros((1, d), F32),
            conv_dw_b=_row(conv_dw_b[l]), conv_ln_g=_row(conv_ln_g[l]), conv_ln_b=_row(conv_ln_b[l]),
            gdn_norm_w=_row(gdn_norm_w[l]), ln1_g=_row(ln1_g[l]), ln1_b=_row(ln1_b[l]), ln2_g=_row(ln2_g[l]), ln2_b=_row(ln2_b[l]),
            hp=jnp.concatenate([jnp.broadcast_to(a_log[l][:, None, None], (nh, 1, LANE)),
                                jnp.broadcast_to(dt_bias[l][:, None, None], (nh, 1, LANE)),
                                jnp.zeros((nh, 6, LANE), F32)], axis=1),
        )

    h = x[0]
    hb16 = h.astype(BF16)
    saved, layers = [], []
    others = _run_side("gather_split", _gather_side(my_flat(0)))[0]
    for l in range(depth):
        layers.append(layer_weights(l, others))
        side = _gather_side(my_flat(l + 1)) if l + 1 < depth else None
        h, hb16, sv, side_out = _layer_fwd(h, hb16, layers[l], alpha, side)
        if side is not None:
            others = side_out[0]
        saved.append(sv)
    loss_blk, dy = _loss_fwd_bwd(h, loss_target[0])
    loss = lax.psum(0.5 * loss_blk[0, 0], ("x", "y", "c"))

    grads = [None] * depth
    shard_grads = [None] * depth

    def finish(l, pair, got):
        shard = _join_halves(_add_chips(pair, got, idx_arr))
        sg, off = {}, 0
        for n, sh, nr in zip(BIG, shard_shapes, rows):
            sg[n] = shard[off:off + nr].reshape(sh)
            off += nr
        shard_grads[l] = sg

    pending = None
    for l in reversed(range(depth)):
        dy, g, reduced = _layer_bwd(dy, saved[l], layers[l], alpha, None if pending is None else pending[1], c_arr)
        if pending is not None:
            finish(pending[0], *reduced)
        grads[l] = g
        parts = []
        for n, sh in zip(BIG, shard_shapes):
            gf = g[n]
            if n in COL_SHARDED:
                gs = jnp.transpose(gf.reshape(sh[0], 4, sh[1]), (1, 0, 2))
            else:
                gs = gf.reshape(4, sh[0], sh[1])
            parts.append(gs.reshape(4, -1, fw))
        if pad_rows:
            parts.append(jnp.zeros((4, pad_rows, fw), F32))
        gflat = jnp.concatenate(parts, axis=1)
        pending = (l, gflat)
    pair = _add_half(pending[1], _run_side("swap_halves", _swap_side(pending[1]))[0], c_arr)
    finish(pending[0], pair, _run_side("scatter_chips", _scatter_side(pair))[0])
    grad_x = dy[None]

    small_parts = [jnp.concatenate([grads[l][n].reshape(-1) for l in range(depth)]) for n in SMALL]
    sizes = [int(p.shape[0]) for p in small_parts]
    flat = jnp.concatenate(small_parts)
    nflat = int(flat.shape[0])
    flat = jnp.pad(flat, (0, (-nflat) % (8 * LANE))).reshape(-1, LANE)
    tot = _sum8(_gather_all(flat)).reshape(-1)
    small_g, off = {}, 0
    for n, sz in zip(SMALL, sizes):
        full = tot[off:off + sz]
        off += sz
        if n == "conv_dw_w":
            full = lax.dynamic_slice_in_dim(full.reshape(depth, kw, 4 * cs), chip * cs, cs, axis=2)
        elif n == "short_conv_w":
            full = lax.dynamic_slice_in_dim(full.reshape(depth, ks, 4 * ss), chip * ss, ss, axis=2)
        small_g[n] = full.reshape(wts[n].shape)

    out_g, out_d, out_m, out_v = {}, {}, {}, {}
    for n in BIG:
        gfull = jnp.stack([shard_grads[l][n] for l in range(depth)])
        sh = wts[n].shape
        two = (sh[0] * sh[1], sh[2])
        dl, nm, nv = _adamw("adamw_" + n, wts[n].reshape(two), gfull.reshape(two), mom[n].reshape(two), var[n].reshape(two))
        out_g[n], out_d[n], out_m[n], out_v[n] = gfull, dl.reshape(sh), nm.reshape(sh), nv.reshape(sh)

    def pack(src):
        v = jnp.concatenate([src[n].reshape(-1) for n in SMALL])
        return jnp.pad(v, (0, (-int(v.shape[0])) % (8 * LANE))).reshape(-1, LANE)

    dl, nm, nv = _adamw("adamw_small", pack(wts), pack(small_g), pack(mom), pack(var))
    off = 0
    for n in SMALL:
        sz = int(np.prod(wts[n].shape))
        out_g[n] = small_g[n]
        out_d[n] = dl.reshape(-1)[off:off + sz].reshape(wts[n].shape)
        out_m[n] = nm.reshape(-1)[off:off + sz].reshape(wts[n].shape)
        out_v[n] = nv.reshape(-1)[off:off + sz].reshape(wts[n].shape)
        off += sz

    return (loss, grad_x, *[out_g[n] for n in ORDER], *[out_d[n] for n in ORDER], *[out_m[n] for n in ORDER],
            *[out_v[n] for n in ORDER])
```
